```python
import math
import jax, jax.numpy as jnp
from jax import lax
import numpy as np

D_MODEL = 2048
BATCH = 1
SEQ = 8192
DEPTH = 4

BLOCK_Q = 128
RMS_EPS = 1e-6
ROPE_THETA = 10000.0
MASK_VALUE = -1e30

DA_HEADS = 8
DA_QK_DIM = 128
DA_V_DIM = 2 * DA_QK_DIM
DA_WIDTH = DA_HEADS * DA_V_DIM
LAMBDA_STD = 0.1

SB_HEADS = 16
SB_HEAD_DIM = 128
SB_WIDTH = SB_HEADS * SB_HEAD_DIM

SEGMENT_SIZES = (
    2 * DA_HEADS * DA_QK_DIM,
    2 * DA_HEADS * DA_QK_DIM,
    DA_WIDTH,
    DA_WIDTH,
    SB_WIDTH,
    SB_WIDTH,
    SB_WIDTH,
    SB_WIDTH,
    D_MODEL,
    D_MODEL,
)
IN_COLS = sum(SEGMENT_SIZES)

kernel_name = "hybrid_diffattn_stickbreaking_gated"


def rms_norm(x, gain):
    xf = x.astype(jnp.float32)
    y = xf * lax.rsqrt(jnp.mean(xf * xf, axis=-1, keepdims=True) + RMS_EPS)
    return (y * gain.astype(jnp.float32)).astype(x.dtype)


def rope(x):
    s_len, d = x.shape[1], x.shape[-1]
    inv_freq = jnp.exp(-(jnp.arange(0, d, 2, dtype=jnp.float32) / d) * math.log(ROPE_THETA))
    ang = jnp.arange(s_len, dtype=jnp.float32)[:, None] * inv_freq[None, :]
    cos = jnp.cos(ang)[None, :, None, :]
    sin = jnp.sin(ang)[None, :, None, :]
    xf = x.astype(jnp.float32)
    x1, x2 = xf[..., : d // 2], xf[..., d // 2:]
    return jnp.concatenate([x1 * cos - x2 * sin, x2 * cos + x1 * sin], axis=-1).astype(x.dtype)


def to_query_blocks(t):
    b, s, h, d = t.shape
    return t.reshape(b, s // BLOCK_Q, BLOCK_Q, h, d).transpose(1, 0, 3, 2, 4)


def from_query_blocks(o):
    nb, b, h, bq, dv = o.shape
    return o.transpose(1, 0, 3, 2, 4).reshape(b, nb * bq, h * dv)


def differential_attention(q1, q2, k1, k2, v, lam):
    s_len = q1.shape[1]
    k1t = k1.transpose(0, 2, 1, 3)
    k2t = k2.transpose(0, 2, 1, 3)
    vt = v.transpose(0, 2, 1, 3)
    kpos = jnp.arange(s_len, dtype=jnp.int32)
    qpos = kpos.reshape(s_len // BLOCK_Q, BLOCK_Q)

    def block(args):
        qb1, qb2, qp = args
        causal = kpos[None, :] <= qp[:, None]

        def probs(qb, kt):
            s = jnp.einsum('bhqd,bhkd->bhqk', qb, kt, preferred_element_type=jnp.float32)
            return jax.nn.softmax(jnp.where(causal, s, MASK_VALUE), axis=-1)

        a = probs(qb1, k1t) - lam * probs(qb2, k2t)
        return jnp.einsum('bhqk,bhkd->bhqd', a.astype(vt.dtype), vt)

    out = lax.map(block, (to_query_blocks(q1), to_query_blocks(q2), qpos))
    return from_query_blocks(out)


def stick_breaking_attention(q, k, v):
    s_len = q.shape[1]
    kt = k.transpose(0, 2, 1, 3)
    vt = v.transpose(0, 2, 1, 3)
    kpos = jnp.arange(s_len, dtype=jnp.int32)
    qpos = kpos.reshape(s_len // BLOCK_Q, BLOCK_Q)

    def block(args):
        qb, qp = args
        strict = kpos[None, :] < qp[:, None]
        z = jnp.einsum('bhqd,bhkd->bhqk', qb, kt, preferred_element_type=jnp.float32)
        log_fail = jnp.where(strict, jax.nn.log_sigmoid(-z), 0.0)
        after = lax.cumsum(log_fail, axis=3, reverse=True) - log_fail
        w = jnp.where(strict, jnp.exp(jax.nn.log_sigmoid(z) + after), 0.0)
        return jnp.einsum('bhqk,bhkd->bhqd', w.astype(vt.dtype), vt)

    out = lax.map(block, (to_query_blocks(q), qpos))
    return from_query_blocks(out)


def split_columns(proj):
    parts, start = [], 0
    for size in SEGMENT_SIZES:
        parts.append(proj[..., start:start + size])
        start += size
    return parts


def hybrid_layer(x, layer_idx, norm_gain, w_in, qk_q_gain, qk_k_gain,
                 lambda_q1, lambda_k1, lambda_q2, lambda_k2, subln_gain,
                 w_branch_a, w_branch_b, w_out):
    b, s, _ = x.shape
    h = rms_norm(x, norm_gain)
    proj = jnp.einsum('bsd,de->bse', h, w_in)
    (da_q, da_k, da_v, da_z, sb_q, sb_k, sb_v, sb_z,
     gate_a, gate_b) = split_columns(proj)

    lambda_init = 0.8 - 0.6 * math.exp(-0.3 * layer_idx)
    q = rope(rms_norm(da_q.reshape(b, s, 2 * DA_HEADS, DA_QK_DIM), qk_q_gain))
    k = rope(rms_norm(da_k.reshape(b, s, 2 * DA_HEADS, DA_QK_DIM), qk_k_gain))
    q = (q * (DA_QK_DIM ** -0.5)).reshape(b, s, DA_HEADS, 2, DA_QK_DIM)
    k = k.reshape(b, s, DA_HEADS, 2, DA_QK_DIM)
    lam = (jnp.exp(jnp.sum(lambda_q1.astype(jnp.float32) * lambda_k1.astype(jnp.float32)))
           - jnp.exp(jnp.sum(lambda_q2.astype(jnp.float32) * lambda_k2.astype(jnp.float32)))
           + lambda_init)
    va = da_v.reshape(b, s, DA_HEADS, DA_V_DIM)
    oa = differential_attention(q[:, :, :, 0, :], q[:, :, :, 1, :],
                                k[:, :, :, 0, :], k[:, :, :, 1, :], va, lam)
    oa = rms_norm(oa.reshape(b, s, DA_HEADS, DA_V_DIM), subln_gain) * (1.0 - lambda_init)
    ua = oa.reshape(b, s, DA_WIDTH) * jax.nn.silu(da_z)

    qs = sb_q.reshape(b, s, SB_HEADS, SB_HEAD_DIM) * (SB_HEAD_DIM ** -0.5)
    ks = sb_k.reshape(b, s, SB_HEADS, SB_HEAD_DIM)
    vs = sb_v.reshape(b, s, SB_HEADS, SB_HEAD_DIM)
    ub = stick_breaking_attention(qs, ks, vs) * jax.nn.silu(sb_z)

    y = (jax.nn.sigmoid(gate_a) * jnp.einsum('bse,ed->bsd', ua, w_branch_a)
         + jax.nn.sigmoid(gate_b) * jnp.einsum('bse,ed->bsd', ub, w_branch_b))
    return x + jnp.einsum('bsd,de->bse', y, w_out)


def setup_inputs(seed: int = 0) -> dict:
    key = jax.random.key(seed)
    ks = jax.random.split(key, 13)
    f32 = jnp.float32
    x = jax.random.normal(ks[0], (BATCH, SEQ, D_MODEL), f32)
    norm_gain = 1.0 + 0.02 * jax.random.normal(ks[1], (DEPTH, D_MODEL), f32)
    w_in = jax.random.normal(ks[2], (DEPTH, D_MODEL, IN_COLS), f32) * D_MODEL ** -0.5
    qk_q_gain = 1.0 + 0.02 * jax.random.normal(ks[3], (DEPTH, DA_QK_DIM), f32)
    qk_k_gain = 1.0 + 0.02 * jax.random.normal(ks[4], (DEPTH, DA_QK_DIM), f32)
    lambda_q1 = LAMBDA_STD * jax.random.normal(ks[5], (DEPTH, DA_QK_DIM), f32)
    lambda_k1 = LAMBDA_STD * jax.random.normal(ks[6], (DEPTH, DA_QK_DIM), f32)
    lambda_q2 = LAMBDA_STD * jax.random.normal(ks[7], (DEPTH, DA_QK_DIM), f32)
    lambda_k2 = LAMBDA_STD * jax.random.normal(ks[8], (DEPTH, DA_QK_DIM), f32)
    subln_gain = 1.0 + 0.02 * jax.random.normal(ks[9], (DEPTH, DA_V_DIM), f32)
    w_branch_a = jax.random.normal(ks[10], (DEPTH, DA_WIDTH, D_MODEL), f32) * DA_WIDTH ** -0.5
    w_branch_b = jax.random.normal(ks[11], (DEPTH, SB_WIDTH, D_MODEL), f32) * SB_WIDTH ** -0.5
    w_out = jax.random.normal(ks[12], (DEPTH, D_MODEL, D_MODEL), f32) * D_MODEL ** -0.5
    return {"x": x, "norm_gain": norm_gain, "w_in": w_in,
            "qk_q_gain": qk_q_gain, "qk_k_gain": qk_k_gain,
            "lambda_q1": lambda_q1, "lambda_k1": lambda_k1,
            "lambda_q2": lambda_q2, "lambda_k2": lambda_k2,
            "subln_gain": subln_gain, "w_branch_a": w_branch_a,
            "w_branch_b": w_branch_b, "w_out": w_out}


def reference(x, norm_gain, w_in, qk_q_gain, qk_k_gain, lambda_q1, lambda_k1,
              lambda_q2, lambda_k2, subln_gain, w_branch_a, w_branch_b, w_out):
    for l in range(DEPTH):
        x = hybrid_layer(x, l, norm_gain[l], w_in[l], qk_q_gain[l], qk_k_gain[l],
                         lambda_q1[l], lambda_k1[l], lambda_q2[l], lambda_k2[l], subln_gain[l],
                         w_branch_a[l], w_branch_b[l], w_out[l])
    return x
```

```python
import functools
import math

import jax
import jax.numpy as jnp
from jax import lax
from jax.experimental import pallas as pl
from jax.experimental.pallas import tpu as pltpu

F32 = jnp.float32
BF16 = jnp.bfloat16

D_MODEL = 2048
RMS_EPS = 1e-6
ROPE_THETA = 10000.0
MASK_VALUE = -1e30

DA_HEADS = 8
DA_QK_DIM = 128
DA_V_DIM = 2 * DA_QK_DIM
DA_WIDTH = DA_HEADS * DA_V_DIM
SB_HEADS = 16
SB_HEAD_DIM = 128
SB_WIDTH = SB_HEADS * SB_HEAD_DIM

COL_DA_Q = 0
COL_DA_K = COL_DA_Q + 2 * DA_HEADS * DA_QK_DIM
COL_DA_V = COL_DA_K + 2 * DA_HEADS * DA_QK_DIM
COL_DA_Z = COL_DA_V + DA_WIDTH
COL_SB_Q = COL_DA_Z + DA_WIDTH
COL_SB_K = COL_SB_Q + SB_WIDTH
COL_SB_V = COL_SB_K + SB_WIDTH
COL_SB_Z = COL_SB_V + SB_WIDTH
COL_GATE = COL_SB_Z + SB_WIDTH
IN_COLS = COL_GATE + 2 * D_MODEL

LANES = 128
VMEM_LIMIT = 56 * 1024 * 1024

NT_DIMS = (((1,), (1,)), ((), ()))


def _params(*sem):
    return pltpu.CompilerParams(dimension_semantics=sem, vmem_limit_bytes=VMEM_LIMIT)


def _norm_kernel(x_ref, g_ref, o_ref):
    x = x_ref[...]
    ms = jnp.mean(x * x, axis=-1, keepdims=True)
    o_ref[...] = (x * lax.rsqrt(ms + RMS_EPS) * g_ref[...]).astype(o_ref.dtype)


def _rms_norm(x, gain, tm):
    s, d = x.shape
    return pl.pallas_call(
        _norm_kernel,
        grid=(s // tm,),
        in_specs=[pl.BlockSpec((tm, d), lambda i: (i, 0)),
                  pl.BlockSpec((1, d), lambda i: (0, 0))],
        out_specs=pl.BlockSpec((tm, d), lambda i: (i, 0)),
        out_shape=jax.ShapeDtypeStruct((s, d), BF16),
        compiler_params=_params("parallel"),
        name="rms_norm",
    )(x, gain.reshape(1, d))


def _proj_qk_kernel(h_ref, w_ref, g_ref, cos_ref, sin_ref, o_ref):
    acc = jnp.dot(h_ref[...], w_ref[...], preferred_element_type=F32)
    g = g_ref[...]
    cos = cos_ref[...]
    sin = sin_ref[...]
    for t in range(o_ref.shape[0]):
        xh = acc[:, t * LANES:(t + 1) * LANES]
        ms = jnp.mean(xh * xh, axis=-1, keepdims=True)
        y = xh * lax.rsqrt(ms + RMS_EPS) * g
        y = y * cos + pltpu.roll(y, LANES // 2, 1) * sin
        o_ref[t] = y.astype(o_ref.dtype)


def _proj_heads_kernel(h_ref, w_ref, o_ref, *, n_scaled, scale):
    acc = jnp.dot(h_ref[...], w_ref[...], preferred_element_type=F32)
    if n_scaled:
        acc = acc * jnp.where(pl.program_id(1) < n_scaled, scale, 1.0).astype(F32)
    hw = o_ref.shape[2]
    for t in range(o_ref.shape[0]):
        o_ref[t] = acc[:, t * hw:(t + 1) * hw].astype(o_ref.dtype)


def _proj_act_kernel(h_ref, w_ref, o_ref, *, act):
    acc = jnp.dot(h_ref[...], w_ref[...], preferred_element_type=F32)
    o_ref[...] = act(acc).astype(o_ref.dtype)


def _proj_call(kernel, h, w_in, layer, col_block_fn, n_tiles, tm, tn, extra_in, extra_specs,
               out_shape, out_spec, name):
    s, d = h.shape
    return pl.pallas_call(
        kernel,
        grid=(s // tm, n_tiles),
        in_specs=[pl.BlockSpec((tm, d), lambda i, j: (i, 0)),
                  pl.BlockSpec((None, d, tn), lambda i, j: (layer, 0, col_block_fn(j)))]
        + extra_specs,
        out_specs=out_spec,
        out_shape=out_shape,
        compiler_params=_params("parallel", "arbitrary"),
        name=name,
    )(h, w_in, *extra_in)


def _da_kernel(sc_ref, lq1_ref, lk1_ref, lq2_ref, lk2_ref,
               q1_ref, q2_ref, k1_ref, k2_ref, v_ref, z_ref, g_ref, o_ref,
               m_ref, l_ref, acc_ref, *, tq, tk):
    i = pl.program_id(1)
    n_sub = tq // tk
    qs = (q1_ref[...], q2_ref[...])
    k_refs = (k1_ref, k2_ref)

    m_ref[...] = jnp.full(m_ref.shape, -jnp.inf, F32)
    l_ref[...] = jnp.zeros(l_ref.shape, F32)
    acc_ref[...] = jnp.zeros(acc_ref.shape, F32)

    def step(j, sub):
        ks = pl.ds(pl.multiple_of(j * tk, tk), tk)
        v = v_ref[ks, :]
        if sub is not None:
            row = lax.broadcasted_iota(jnp.int32, (tq, tk), 0)
            col = lax.broadcasted_iota(jnp.int32, (tq, tk), 1) + sub * tk
            causal = col <= row
        for st in range(2):
            s = lax.dot_general(qs[st], k_refs[st][ks, :], NT_DIMS, preferred_element_type=F32)
            if sub is not None:
                s = jnp.where(causal, s, MASK_VALUE)
            m_old = m_ref[st]
            m_new = jnp.maximum(m_old, jnp.max(s, axis=-1, keepdims=True))
            alpha = jnp.exp(m_old - m_new)
            p = jnp.exp(s - m_new)
            l_ref[st] = alpha * l_ref[st] + jnp.sum(p, axis=-1, keepdims=True)
            acc_ref[st] = alpha * acc_ref[st] + jnp.dot(p.astype(BF16), v, preferred_element_type=F32)
            m_ref[st] = m_new

    def body(j, c):
        step(j, None)
        return c

    lax.fori_loop(0, i * n_sub, body, 0)
    for sub in range(n_sub):
        step(i * n_sub + sub, sub)

    lambda_init = sc_ref[0]
    lam = (jnp.exp(jnp.sum(lq1_ref[...] * lk1_ref[...], axis=-1, keepdims=True))
           - jnp.exp(jnp.sum(lq2_ref[...] * lk2_ref[...], axis=-1, keepdims=True))
           + lambda_init)
    o = acc_ref[0] / l_ref[0] - lam * (acc_ref[1] / l_ref[1])
    ms = jnp.mean(o * o, axis=-1, keepdims=True)
    o = o * lax.rsqrt(ms + RMS_EPS) * g_ref[...] * (1.0 - lambda_init)
    o_ref[...] = (o * z_ref[...]).astype(o_ref.dtype)


def _diff_attention(scalars, lams, qk, v, zact, subln_gain, tq, tk):
    _, s, _ = qk.shape
    nh = DA_HEADS
    row = lambda h, i: (0, 0)
    kern = functools.partial(_da_kernel, tq=tq, tk=tk)
    return pl.pallas_call(
        kern,
        grid=(nh, s // tq),
        in_specs=[pl.BlockSpec(memory_space=pltpu.SMEM)]
        + [pl.BlockSpec((1, DA_QK_DIM), row)] * 4
        + [pl.BlockSpec((None, tq, DA_QK_DIM), lambda h, i: (2 * h, i, 0)),
           pl.BlockSpec((None, tq, DA_QK_DIM), lambda h, i: (2 * h + 1, i, 0)),
           pl.BlockSpec((None, s, DA_QK_DIM), lambda h, i: (2 * nh + 2 * h, 0, 0)),
           pl.BlockSpec((None, s, DA_QK_DIM), lambda h, i: (2 * nh + 2 * h + 1, 0, 0)),
           pl.BlockSpec((None, s, DA_V_DIM), lambda h, i: (h, 0, 0)),
           pl.BlockSpec((tq, DA_V_DIM), lambda h, i: (i, h)),
           pl.BlockSpec((1, DA_V_DIM), row)],
        out_specs=pl.BlockSpec((tq, DA_V_DIM), lambda h, i: (i, h)),
        out_shape=jax.ShapeDtypeStruct((s, DA_WIDTH), BF16),
        scratch_shapes=[pltpu.VMEM((2, tq, 1), F32),
                        pltpu.VMEM((2, tq, 1), F32),
                        pltpu.VMEM((2, tq, DA_V_DIM), F32)],
        compiler_params=_params("parallel", "arbitrary"),
        name="diff_attention",
    )(scalars, *lams, qk, qk, qk, qk, v, zact, subln_gain.reshape(1, DA_V_DIM))


def _sb_kernel(q_ref, k_ref, v_ref, z_ref, o_ref, c_ref, acc_ref, *, tq, tk):
    i = pl.program_id(1)
    n_sub = tq // tk
    q = q_ref[...]
    tri = (lax.broadcasted_iota(jnp.int32, (tk, tk), 0)
           > lax.broadcasted_iota(jnp.int32, (tk, tk), 1)).astype(BF16)

    c_ref[...] = jnp.zeros(c_ref.shape, F32)
    acc_ref[...] = jnp.zeros(acc_ref.shape, F32)

    def step(j, sub):
        ks = pl.ds(pl.multiple_of(j * tk, tk), tk)
        z = lax.dot_general(q, k_ref[ks, :], NT_DIMS, preferred_element_type=F32)
        sp = jnp.maximum(z, 0.0) + jnp.log(1.0 + jnp.exp(-jnp.abs(z)))
        lsz = z - sp
        if sub is not None:
            row = lax.broadcasted_iota(jnp.int32, (tq, tk), 0)
            col = lax.broadcasted_iota(jnp.int32, (tq, tk), 1) + sub * tk
            strict = col < row
            sp = jnp.where(strict, sp, 0.0)
        hi = sp.astype(BF16)
        lo = (sp - hi.astype(F32)).astype(BF16)
        cs = (jnp.dot(hi, tri, preferred_element_type=F32)
              + jnp.dot(lo, tri, preferred_element_type=F32))
        c = c_ref[...]
        w = jnp.exp(lsz - cs - c)
        if sub is not None:
            w = jnp.where(strict, w, 0.0)
        acc_ref[...] += jnp.dot(w.astype(BF16), v_ref[ks, :], preferred_element_type=F32)
        c_ref[...] = c + cs[:, 0:1] + sp[:, 0:1]

    for sub in reversed(range(n_sub)):
        step(i * n_sub + sub, sub)

    n_full = i * n_sub

    def body(jj, c):
        step(n_full - 1 - jj, None)
        return c

    lax.fori_loop(0, n_full, body, 0)
    o_ref[...] = (acc_ref[...] * z_ref[...]).astype(o_ref.dtype)


def _sb_attention(qkv, zact, tq, tk):
    _, s, _ = qkv.shape
    nh = SB_HEADS
    hd = SB_HEAD_DIM
    z_off = DA_WIDTH // hd
    kern = functools.partial(_sb_kernel, tq=tq, tk=tk)
    return pl.pallas_call(
        kern,
        grid=(nh, s // tq),
        in_specs=[pl.BlockSpec((None, tq, hd), lambda h, i: (h, i, 0)),
                  pl.BlockSpec((None, s, hd), lambda h, i: (nh + h, 0, 0)),
                  pl.BlockSpec((None, s, hd), lambda h, i: (2 * nh + h, 0, 0)),
                  pl.BlockSpec((tq, hd), lambda h, i: (i, z_off + h))],
        out_specs=pl.BlockSpec((tq, hd), lambda h, i: (i, h)),
        out_shape=jax.ShapeDtypeStruct((s, SB_WIDTH), BF16),
        scratch_shapes=[pltpu.VMEM((tq, 1), F32),
                        pltpu.VMEM((tq, hd), F32)],
        compiler_params=_params("parallel", "arbitrary"),
        name="sb_attention",
    )(qkv, qkv, qkv, zact)


def _merge_kernel(ua_ref, ub_ref, wa_ref, wb_ref, ga_ref, gb_ref, o_ref):
    ya = jnp.dot(ua_ref[...], wa_ref[...], preferred_element_type=F32)
    yb = jnp.dot(ub_ref[...], wb_ref[...], preferred_element_type=F32)
    o_ref[...] = (ga_ref[...] * ya + gb_ref[...] * yb).astype(o_ref.dtype)


def _merge(ua, ub, wa, wb, gates, layer, tm, tn):
    s, d = ua.shape
    nj = D_MODEL // tn
    return pl.pallas_call(
        _merge_kernel,
        grid=(s // tm, nj),
        in_specs=[pl.BlockSpec((tm, d), lambda i, j: (i, 0)),
                  pl.BlockSpec((tm, d), lambda i, j: (i, 0)),
                  pl.BlockSpec((None, d, tn), lambda i, j: (layer, 0, j)),
                  pl.BlockSpec((None, d, tn), lambda i, j: (layer, 0, j)),
                  pl.BlockSpec((tm, tn), lambda i, j: (i, j)),
                  pl.BlockSpec((tm, tn), lambda i, j: (i, nj + j))],
        out_specs=pl.BlockSpec((tm, tn), lambda i, j: (i, j)),
        out_shape=jax.ShapeDtypeStruct((s, D_MODEL), BF16),
        compiler_params=_params("parallel", "arbitrary"),
        name="gated_merge",
    )(ua, ub, wa, wb, gates, gates)


def _out_kernel(y_ref, w_ref, x_ref, o_ref):
    o_ref[...] = x_ref[...] + jnp.dot(y_ref[...], w_ref[...], preferred_element_type=F32)


def _out_proj(y, w_out, x, layer, tm, tn):
    s, d = y.shape
    return pl.pallas_call(
        _out_kernel,
        grid=(s // tm, D_MODEL // tn),
        in_specs=[pl.BlockSpec((tm, d), lambda i, j: (i, 0)),
                  pl.BlockSpec((None, d, tn), lambda i, j: (layer, 0, j)),
                  pl.BlockSpec((tm, tn), lambda i, j: (i, j))],
        out_specs=pl.BlockSpec((tm, tn), lambda i, j: (i, j)),
        out_shape=jax.ShapeDtypeStruct((s, D_MODEL), F32),
        compiler_params=_params("parallel", "arbitrary"),
        name="out_proj",
    )(y, w_out, x)


def _silu(x):
    return x * (1.0 / (1.0 + jnp.exp(-x)))


def _sigmoid(x):
    return 1.0 / (1.0 + jnp.exp(-x))


def _tiles(s):
    tm = min(s, 1024)
    tq = min(s, 512)
    return dict(tm=tm, tn=512, tq=tq, da_tk=tq, sb_tk=min(tq, 256))


def kernel(x, norm_gain, w_in, qk_q_gain, qk_k_gain, lambda_q1, lambda_k1, lambda_q2, lambda_k2,
           subln_gain, w_branch_a, w_branch_b, w_out):
    b, s, d = x.shape
    assert b == 1 and d == D_MODEL and w_in.shape[2] == IN_COLS
    depth = w_in.shape[0]
    t = _tiles(s)
    tm, tn, tq = t["tm"], t["tn"], t["tq"]
    assert s % tm == 0 and s % tq == 0

    w_in_b = w_in.astype(BF16)
    wa_b = w_branch_a.astype(BF16)
    wb_b = w_branch_b.astype(BF16)
    wo_b = w_out.astype(BF16)

    hd = DA_QK_DIM
    inv_freq = jnp.exp(-(jnp.arange(0, hd, 2, dtype=F32) / hd) * math.log(ROPE_THETA))
    ang = jnp.arange(s, dtype=F32)[:, None] * inv_freq[None, :]
    cos2 = jnp.concatenate([jnp.cos(ang), jnp.cos(ang)], axis=-1)
    sin2 = jnp.concatenate([-jnp.sin(ang), jnp.sin(ang)], axis=-1)

    xs = x.reshape(s, d)
    for l in range(depth):
        lambda_init = 0.8 - 0.6 * math.exp(-0.3 * l)
        h = _rms_norm(xs, norm_gain[l], tm)

        nqk = 2 * DA_HEADS * DA_QK_DIM // tn
        gains = jnp.stack([qk_q_gain[l] * (DA_QK_DIM ** -0.5), qk_k_gain[l]]).reshape(2, 1, hd)
        qk = _proj_call(
            _proj_qk_kernel, h, w_in_b, l, lambda j: COL_DA_Q // tn + j, 2 * nqk, tm, tn,
            [gains, cos2, sin2],
            [pl.BlockSpec((None, 1, hd), lambda i, j: (j // nqk, 0, 0)),
             pl.BlockSpec((tm, hd), lambda i, j: (i, 0)),
             pl.BlockSpec((tm, hd), lambda i, j: (i, 0))],
            jax.ShapeDtypeStruct((4 * DA_HEADS, s, hd), BF16),
            pl.BlockSpec((tn // hd, tm, hd), lambda i, j: (j, i, 0)),
            "proj_da_qk")
        v_da = _proj_call(
            functools.partial(_proj_heads_kernel, n_scaled=0, scale=1.0),
            h, w_in_b, l, lambda j: COL_DA_V // tn + j, DA_WIDTH // tn, tm, tn, [], [],
            jax.ShapeDtypeStruct((DA_HEADS, s, DA_V_DIM), BF16),
            pl.BlockSpec((tn // DA_V_DIM, tm, DA_V_DIM), lambda i, j: (j, i, 0)),
            "proj_da_v")
        qkv_sb = _proj_call(
            functools.partial(_proj_heads_kernel, n_scaled=SB_WIDTH // tn,
                              scale=SB_HEAD_DIM ** -0.5),
            h, w_in_b, l, lambda j: COL_SB_Q // tn + j, 3 * SB_WIDTH // tn, tm, tn, [], [],
            jax.ShapeDtypeStruct((3 * SB_HEADS, s, SB_HEAD_DIM), BF16),
            pl.BlockSpec((tn // SB_HEAD_DIM, tm, SB_HEAD_DIM), lambda i, j: (j, i, 0)),
            "proj_sb_qkv")
        nz = DA_WIDTH // tn
        zact = _proj_call(
            functools.partial(_proj_act_kernel, act=_silu),
            h, w_in_b, l,
            lambda j: jnp.where(j < nz, COL_DA_Z // tn + j, COL_SB_Z // tn + (j - nz)),
            nz + SB_WIDTH // tn, tm, tn, [], [],
            jax.ShapeDtypeStruct((s, DA_WIDTH + SB_WIDTH), F32),
            pl.BlockSpec((tm, tn), lambda i, j: (i, j)),
            "proj_silu")
        gates = _proj_call(
            functools.partial(_proj_act_kernel, act=_sigmoid),
            h, w_in_b, l, lambda j: COL_GATE // tn + j, 2 * D_MODEL // tn, tm, tn, [], [],
            jax.ShapeDtypeStruct((s, 2 * D_MODEL), F32),
            pl.BlockSpec((tm, tn), lambda i, j: (i, j)),
            "proj_gates")

        scalars = jnp.array([lambda_init], F32)
        lams = [a[l].reshape(1, hd) for a in (lambda_q1, lambda_k1, lambda_q2, lambda_k2)]
        ua = _diff_attention(scalars, lams, qk, v_da, zact, subln_gain[l], tq, t["da_tk"])
        ub = _sb_attention(qkv_sb, zact, tq, t["sb_tk"])

        y = _merge(ua, ub, wa_b, wb_b, gates, l, tm, tn)
        xs = _out_proj(y, wo_b, xs, l, tm, tn)
    return xs.reshape(b, s, d)
```

```python
import functools
import math

import jax
import jax.numpy as jnp
from jax import lax
from jax.experimental import pallas as pl
from jax.experimental.pallas import tpu as pltpu

F32 = jnp.float32
BF16 = jnp.bfloat16

D_MODEL = 2048
RMS_EPS = 1e-6
ROPE_THETA = 10000.0
MASK_VALUE = -1e30

DA_HEADS = 8
DA_QK_DIM = 128
DA_V_DIM = 2 * DA_QK_DIM
DA_WIDTH = DA_HEADS * DA_V_DIM
SB_HEADS = 16
SB_HEAD_DIM = 128
SB_WIDTH = SB_HEADS * SB_HEAD_DIM

COL_DA_Q = 0
COL_DA_K = COL_DA_Q + 2 * DA_HEADS * DA_QK_DIM
COL_DA_V = COL_DA_K + 2 * DA_HEADS * DA_QK_DIM
COL_DA_Z = COL_DA_V + DA_WIDTH
COL_SB_Q = COL_DA_Z + DA_WIDTH
COL_SB_K = COL_SB_Q + SB_WIDTH
COL_SB_V = COL_SB_K + SB_WIDTH
COL_SB_Z = COL_SB_V + SB_WIDTH
COL_GATE = COL_SB_Z + SB_WIDTH
IN_COLS = COL_GATE + 2 * D_MODEL

LANES = 128
MXU_DIM = 256
LOG2E = math.log2(math.e)
SOFTPLUS_LINEAR_ABOVE = 64.0
VMEM_LIMIT = 56 * 1024 * 1024

NT_DIMS = (((1,), (1,)), ((), ()))


def _params(*sem):
    return pltpu.CompilerParams(dimension_semantics=sem, vmem_limit_bytes=VMEM_LIMIT)


def _norm_kernel(x_ref, g_ref, o_ref):
    x = x_ref[...]
    ms = jnp.mean(x * x, axis=-1, keepdims=True)
    o_ref[...] = (x * lax.rsqrt(ms + RMS_EPS) * g_ref[...]).astype(o_ref.dtype)


def _rms_norm(x, gain, tm):
    s, d = x.shape
    return pl.pallas_call(
        _norm_kernel,
        grid=(s // tm,),
        in_specs=[pl.BlockSpec((tm, d), lambda i: (i, 0)),
                  pl.BlockSpec((1, d), lambda i: (0, 0))],
        out_specs=pl.BlockSpec((tm, d), lambda i: (i, 0)),
        out_shape=jax.ShapeDtypeStruct((s, d), BF16),
        compiler_params=_params("parallel"),
        name="rms_norm",
    )(x, gain.reshape(1, d))


def _proj_qk_kernel(h_ref, w_ref, g_ref, cos_ref, sin_ref, o_ref):
    acc = jnp.dot(h_ref[...], w_ref[...], preferred_element_type=F32)
    g = g_ref[...]
    cos = cos_ref[...]
    sin = sin_ref[...]
    for t in range(o_ref.shape[0]):
        xh = acc[:, t * LANES:(t + 1) * LANES]
        ms = jnp.mean(xh * xh, axis=-1, keepdims=True)
        y = xh * lax.rsqrt(ms + RMS_EPS) * g
        y = y * cos + pltpu.roll(y, LANES // 2, 1) * sin
        o_ref[t] = y.astype(o_ref.dtype)


def _proj_heads_kernel(h_ref, w_ref, o_ref, *, n_scaled, scale):
    acc = jnp.dot(h_ref[...], w_ref[...], preferred_element_type=F32)
    if n_scaled:
        acc = acc * jnp.where(pl.program_id(1) < n_scaled, scale, 1.0).astype(F32)
    hw = o_ref.shape[2]
    for t in range(o_ref.shape[0]):
        o_ref[t] = acc[:, t * hw:(t + 1) * hw].astype(o_ref.dtype)


def _proj_act_kernel(h_ref, w_ref, o_ref, *, act):
    acc = jnp.dot(h_ref[...], w_ref[...], preferred_element_type=F32)
    o_ref[...] = act(acc).astype(o_ref.dtype)


def _proj_call(kernel, h, w_in, layer, col_block_fn, n_tiles, tm, tn, extra_in, extra_specs,
               out_shape, out_spec, name):
    s, d = h.shape
    return pl.pallas_call(
        kernel,
        grid=(s // tm, n_tiles),
        in_specs=[pl.BlockSpec((tm, d), lambda i, j: (i, 0)),
                  pl.BlockSpec((None, d, tn), lambda i, j: (layer, 0, col_block_fn(j)))]
        + extra_specs,
        out_specs=out_spec,
        out_shape=out_shape,
        compiler_params=_params("parallel", "arbitrary"),
        name=name,
    )(h, w_in, *extra_in)


def _da_kernel(sc_ref, lq1_ref, lk1_ref, lq2_ref, lk2_ref,
               q1_ref, q2_ref, k1_ref, k2_ref, v_ref, z_ref, g_ref, o_ref,
               m_ref, l_ref, acc_ref, *, tq, tk):
    i = pl.program_id(1)
    n_sub = tq // tk
    qs = (q1_ref[...], q2_ref[...])
    k_refs = (k1_ref, k2_ref)

    m_ref[...] = jnp.full(m_ref.shape, -jnp.inf, F32)
    l_ref[...] = jnp.zeros(l_ref.shape, F32)
    acc_ref[...] = jnp.zeros(acc_ref.shape, F32)

    def step(j, sub):
        ks = pl.ds(pl.multiple_of(j * tk, tk), tk)
        v = v_ref[ks, :]
        if sub is not None:
            row = lax.broadcasted_iota(jnp.int32, (tq, tk), 0)
            col = lax.broadcasted_iota(jnp.int32, (tq, tk), 1) + sub * tk
            causal = col <= row
        for st in range(2):
            s = lax.dot_general(qs[st], k_refs[st][ks, :], NT_DIMS, preferred_element_type=F32)
            if sub is not None:
                s = jnp.where(causal, s, MASK_VALUE)
            sg = [s[:, g * LANES:(g + 1) * LANES] for g in range(tk // LANES)]
            gmax = functools.reduce(jnp.maximum, sg)
            m_old = m_ref[st]
            m_new = jnp.maximum(m_old, jnp.max(gmax, axis=-1, keepdims=True))
            alpha = jnp.exp2(m_old - m_new)
            pg = [jnp.exp2(x - m_new) for x in sg]
            l_ref[st] = alpha * l_ref[st] + functools.reduce(jnp.add, pg)
            p = jnp.concatenate([x.astype(BF16) for x in pg], axis=1)
            pv = jnp.dot(p, v, preferred_element_type=F32)
            acc_ref[st] = jnp.concatenate([alpha] * (DA_V_DIM // LANES), axis=1) * acc_ref[st] + pv
            m_ref[st] = m_new

    def body(j, c):
        step(j, None)
        return c

    lax.fori_loop(0, i * n_sub, body, 0)
    for sub in range(n_sub):
        step(i * n_sub + sub, sub)

    lambda_init = sc_ref[0]
    lam = (jnp.exp(jnp.sum(lq1_ref[...] * lk1_ref[...], axis=-1, keepdims=True))
           - jnp.exp(jnp.sum(lq2_ref[...] * lk2_ref[...], axis=-1, keepdims=True))
           + lambda_init)
    l1 = jnp.sum(l_ref[0], axis=-1, keepdims=True)
    l2 = jnp.sum(l_ref[1], axis=-1, keepdims=True)
    o = acc_ref[0] / l1 - lam * (acc_ref[1] / l2)
    ms = jnp.mean(o * o, axis=-1, keepdims=True)
    o = o * lax.rsqrt(ms + RMS_EPS) * g_ref[...] * (1.0 - lambda_init)
    o_ref[...] = (o * z_ref[...]).astype(o_ref.dtype)


def _diff_attention(scalars, lams, qk, v, zact, subln_gain, tq, tk):
    _, s, _ = qk.shape
    nh = DA_HEADS
    row = lambda h, i: (0, 0)
    kern = functools.partial(_da_kernel, tq=tq, tk=tk)
    return pl.pallas_call(
        kern,
        grid=(nh, s // tq),
        in_specs=[pl.BlockSpec(memory_space=pltpu.SMEM)]
        + [pl.BlockSpec((1, DA_QK_DIM), row)] * 4
        + [pl.BlockSpec((None, tq, DA_QK_DIM), lambda h, i: (2 * h, i, 0)),
           pl.BlockSpec((None, tq, DA_QK_DIM), lambda h, i: (2 * h + 1, i, 0)),
           pl.BlockSpec((None, s, DA_QK_DIM), lambda h, i: (2 * nh + 2 * h, 0, 0)),
           pl.BlockSpec((None, s, DA_QK_DIM), lambda h, i: (2 * nh + 2 * h + 1, 0, 0)),
           pl.BlockSpec((None, s, DA_V_DIM), lambda h, i: (h, 0, 0)),
           pl.BlockSpec((tq, DA_V_DIM), lambda h, i: (i, h)),
           pl.BlockSpec((1, DA_V_DIM), row)],
        out_specs=pl.BlockSpec((tq, DA_V_DIM), lambda h, i: (i, h)),
        out_shape=jax.ShapeDtypeStruct((s, DA_WIDTH), BF16),
        scratch_shapes=[pltpu.VMEM((2, tq, LANES), F32),
                        pltpu.VMEM((2, tq, LANES), F32),
                        pltpu.VMEM((2, tq, DA_V_DIM), F32)],
        compiler_params=_params("parallel", "arbitrary"),
        name="diff_attention",
    )(scalars, *lams, qk, qk, qk, qk, v, zact, subln_gain.reshape(1, DA_V_DIM))


def _sb_kernel(q_ref, k_ref, v_ref, z_ref, o_ref, c_ref, acc_ref, *, tq, tk, cw):
    i = pl.program_id(1)
    n_sub = tq // tk
    q = q_ref[...]
    tri = (lax.broadcasted_iota(jnp.int32, (cw, cw), 0)
           >= lax.broadcasted_iota(jnp.int32, (cw, cw), 1)).astype(BF16)

    c_ref[...] = jnp.zeros(c_ref.shape, F32)
    acc_ref[...] = jnp.zeros(acc_ref.shape, F32)

    def step(j, sub):
        ks = pl.ds(pl.multiple_of(j * tk, tk), tk)
        z = lax.dot_general(q, k_ref[ks, :], NT_DIMS, preferred_element_type=F32)
        sp = jnp.where(z > SOFTPLUS_LINEAR_ABOVE, z, jnp.log2(1.0 + jnp.exp2(z)))
        if sub is not None:
            row = lax.broadcasted_iota(jnp.int32, (tq, tk), 0)
            col = lax.broadcasted_iota(jnp.int32, (tq, tk), 1) + sub * tk
            strict = col < row
            sp = jnp.where(strict, sp, 0.0)
        c = c_ref[...]
        ws = []
        for b in reversed(range(tk // cw)):
            blk = slice(b * cw, (b + 1) * cw)
            cs = jnp.dot(sp[:, blk].astype(BF16), tri, preferred_element_type=F32)
            w = jnp.exp2(z[:, blk] - cs - c)
            if sub is not None:
                w = jnp.where(strict[:, blk], w, 0.0)
            ws.append(w.astype(BF16))
            c = c + cs[:, 0:1]
        w = jnp.concatenate(ws[::-1], axis=1)
        acc_ref[...] += jnp.dot(w, v_ref[ks, :], preferred_element_type=F32)
        c_ref[...] = c

    for sub in reversed(range(n_sub)):
        step(i * n_sub + sub, sub)

    n_full = i * n_sub

    def body(jj, c):
        step(n_full - 1 - jj, None)
        return c

    lax.fori_loop(0, n_full, body, 0)
    o_ref[...] = (acc_ref[...] * z_ref[...]).astype(o_ref.dtype)


def _sb_attention(qkv, zact, tq, tk, cw):
    _, s, _ = qkv.shape
    nh = SB_HEADS
    hd = SB_HEAD_DIM
    z_off = DA_WIDTH // hd
    kern = functools.partial(_sb_kernel, tq=tq, tk=tk, cw=cw)
    return pl.pallas_call(
        kern,
        grid=(nh, s // tq),
        in_specs=[pl.BlockSpec((None, tq, hd), lambda h, i: (h, i, 0)),
                  pl.BlockSpec((None, s, hd), lambda h, i: (nh + h, 0, 0)),
                  pl.BlockSpec((None, s, hd), lambda h, i: (2 * nh + h, 0, 0)),
                  pl.BlockSpec((tq, hd), lambda h, i: (i, z_off + h))],
        out_specs=pl.BlockSpec((tq, hd), lambda h, i: (i, h)),
        out_shape=jax.ShapeDtypeStruct((s, SB_WIDTH), BF16),
        scratch_shapes=[pltpu.VMEM((tq, 1), F32),
                        pltpu.VMEM((tq, hd), F32)],
        compiler_params=_params("parallel", "arbitrary"),
        name="sb_attention",
    )(qkv, qkv, qkv, zact)


def _merge_kernel(ua_ref, ub_ref, wa_ref, wb_ref, ga_ref, gb_ref, o_ref):
    ya = jnp.dot(ua_ref[...], wa_ref[...], preferred_element_type=F32)
    yb = jnp.dot(ub_ref[...], wb_ref[...], preferred_element_type=F32)
    o_ref[...] = (ga_ref[...] * ya + gb_ref[...] * yb).astype(o_ref.dtype)


def _merge(ua, ub, wa, wb, gates, layer, tm, tn):
    s, d = ua.shape
    nj = D_MODEL // tn
    return pl.pallas_call(
        _merge_kernel,
        grid=(s // tm, nj),
        in_specs=[pl.BlockSpec((tm, d), lambda i, j: (i, 0)),
                  pl.BlockSpec((tm, d), lambda i, j: (i, 0)),
                  pl.BlockSpec((None, d, tn), lambda i, j: (layer, 0, j)),
                  pl.BlockSpec((None, d, tn), lambda i, j: (layer, 0, j)),
                  pl.BlockSpec((tm, tn), lambda i, j: (i, j)),
                  pl.BlockSpec((tm, tn), lambda i, j: (i, nj + j))],
        out_specs=pl.BlockSpec((tm, tn), lambda i, j: (i, j)),
        out_shape=jax.ShapeDtypeStruct((s, D_MODEL), BF16),
        compiler_params=_params("parallel", "arbitrary"),
        name="gated_merge",
    )(ua, ub, wa, wb, gates, gates)


def _out_kernel(y_ref, w_ref, x_ref, o_ref):
    o_ref[...] = x_ref[...] + jnp.dot(y_ref[...], w_ref[...], preferred_element_type=F32)


def _out_proj(y, w_out, x, layer, tm, tn):
    s, d = y.shape
    return pl.pallas_call(
        _out_kernel,
        grid=(s // tm, D_MODEL // tn),
        in_specs=[pl.BlockSpec((tm, d), lambda i, j: (i, 0)),
                  pl.BlockSpec((None, d, tn), lambda i, j: (layer, 0, j)),
                  pl.BlockSpec((tm, tn), lambda i, j: (i, j))],
        out_specs=pl.BlockSpec((tm, tn), lambda i, j: (i, j)),
        out_shape=jax.ShapeDtypeStruct((s, D_MODEL), F32),
        compiler_params=_params("parallel", "arbitrary"),
        name="out_proj",
    )(y, w_out, x)


def _silu(x):
    return x * (1.0 / (1.0 + jnp.exp(-x)))


def _sigmoid(x):
    return 1.0 / (1.0 + jnp.exp(-x))


def _tiles(s):
    tm = min(s, 1024)
    tq = min(s, 1024)
    tk = min(tq, 512)
    return dict(tm=tm, tn=512, tq=tq, da_tk=tk, sb_tk=tk, sb_cw=min(tk, MXU_DIM))


def kernel(x, norm_gain, w_in, qk_q_gain, qk_k_gain, lambda_q1, lambda_k1, lambda_q2, lambda_k2,
           subln_gain, w_branch_a, w_branch_b, w_out):
    b, s, d = x.shape
    assert b == 1 and d == D_MODEL and w_in.shape[2] == IN_COLS
    depth = w_in.shape[0]
    t = _tiles(s)
    tm, tn, tq = t["tm"], t["tn"], t["tq"]
    assert s % tm == 0 and s % tq == 0

    w_in_b = w_in.astype(BF16)
    wa_b = w_branch_a.astype(BF16)
    wb_b = w_branch_b.astype(BF16)
    wo_b = w_out.astype(BF16)

    hd = DA_QK_DIM
    inv_freq = jnp.exp(-(jnp.arange(0, hd, 2, dtype=F32) / hd) * math.log(ROPE_THETA))
    ang = jnp.arange(s, dtype=F32)[:, None] * inv_freq[None, :]
    cos2 = jnp.concatenate([jnp.cos(ang), jnp.cos(ang)], axis=-1)
    sin2 = jnp.concatenate([-jnp.sin(ang), jnp.sin(ang)], axis=-1)

    xs = x.reshape(s, d)
    for l in range(depth):
        lambda_init = 0.8 - 0.6 * math.exp(-0.3 * l)
        h = _rms_norm(xs, norm_gain[l], tm)

        nqk = 2 * DA_HEADS * DA_QK_DIM // tn
        gains = jnp.stack([qk_q_gain[l] * (DA_QK_DIM ** -0.5 * LOG2E),
                           qk_k_gain[l]]).reshape(2, 1, hd)
        qk = _proj_call(
            _proj_qk_kernel, h, w_in_b, l, lambda j: COL_DA_Q // tn + j, 2 * nqk, tm, tn,
            [gains, cos2, sin2],
            [pl.BlockSpec((None, 1, hd), lambda i, j: (j // nqk, 0, 0)),
             pl.BlockSpec((tm, hd), lambda i, j: (i, 0)),
             pl.BlockSpec((tm, hd), lambda i, j: (i, 0))],
            jax.ShapeDtypeStruct((4 * DA_HEADS, s, hd), BF16),
            pl.BlockSpec((tn // hd, tm, hd), lambda i, j: (j, i, 0)),
            "proj_da_qk")
        v_da = _proj_call(
            functools.partial(_proj_heads_kernel, n_scaled=0, scale=1.0),
            h, w_in_b, l, lambda j: COL_DA_V // tn + j, DA_WIDTH // tn, tm, tn, [], [],
            jax.ShapeDtypeStruct((DA_HEADS, s, DA_V_DIM), BF16),
            pl.BlockSpec((tn // DA_V_DIM, tm, DA_V_DIM), lambda i, j: (j, i, 0)),
            "proj_da_v")
        qkv_sb = _proj_call(
            functools.partial(_proj_heads_kernel, n_scaled=SB_WIDTH // tn,
                              scale=SB_HEAD_DIM ** -0.5 * LOG2E),
            h, w_in_b, l, lambda j: COL_SB_Q // tn + j, 3 * SB_WIDTH // tn, tm, tn, [], [],
            jax.ShapeDtypeStruct((3 * SB_HEADS, s, SB_HEAD_DIM), BF16),
            pl.BlockSpec((tn // SB_HEAD_DIM, tm, SB_HEAD_DIM), lambda i, j: (j, i, 0)),
            "proj_sb_qkv")
        nz = DA_WIDTH // tn
        zact = _proj_call(
            functools.partial(_proj_act_kernel, act=_silu),
            h, w_in_b, l,
            lambda j: jnp.where(j < nz, COL_DA_Z // tn + j, COL_SB_Z // tn + (j - nz)),
            nz + SB_WIDTH // tn, tm, tn, [], [],
            jax.ShapeDtypeStruct((s, DA_WIDTH + SB_WIDTH), F32),
            pl.BlockSpec((tm, tn), lambda i, j: (i, j)),
            "proj_silu")
        gates = _proj_call(
            functools.partial(_proj_act_kernel, act=_sigmoid),
            h, w_in_b, l, lambda j: COL_GATE // tn + j, 2 * D_MODEL // tn, tm, tn, [], [],
            jax.ShapeDtypeStruct((s, 2 * D_MODEL), F32),
            pl.BlockSpec((tm, tn), lambda i, j: (i, j)),
            "proj_gates")

        scalars = jnp.array([lambda_init], F32)
        lams = [a[l].reshape(1, hd) for a in (lambda_q1, lambda_k1, lambda_q2, lambda_k2)]
        ua = _diff_attention(scalars, lams, qk, v_da, zact, subln_gain[l], tq, t["da_tk"])
        ub = _sb_attention(qkv_sb, zact, tq, t["sb_tk"], t["sb_cw"])

        y = _merge(ua, ub, wa_b, wb_b, gates, l, tm, tn)
        xs = _out_proj(y, wo_b, xs, l, tm, tn)
    return xs.reshape(b, s, d)
```

```python
import functools
import math

import jax
import jax.numpy as jnp
from jax import lax
from jax.experimental import pallas as pl
from jax.experimental.pallas import tpu as pltpu

F32 = jnp.float32
BF16 = jnp.bfloat16

D_MODEL = 2048
RMS_EPS = 1e-6
ROPE_THETA = 10000.0
MASK_VALUE = -1e30

DA_HEADS = 8
DA_QK_DIM = 128
DA_V_DIM = 2 * DA_QK_DIM
DA_WIDTH = DA_HEADS * DA_V_DIM
SB_HEADS = 16
SB_HEAD_DIM = 128
SB_WIDTH = SB_HEADS * SB_HEAD_DIM

COL_DA_Q = 0
COL_DA_K = COL_DA_Q + 2 * DA_HEADS * DA_QK_DIM
COL_DA_V = COL_DA_K + 2 * DA_HEADS * DA_QK_DIM
COL_DA_Z = COL_DA_V + DA_WIDTH
COL_SB_Q = COL_DA_Z + DA_WIDTH
COL_SB_K = COL_SB_Q + SB_WIDTH
COL_SB_V = COL_SB_K + SB_WIDTH
COL_SB_Z = COL_SB_V + SB_WIDTH
COL_GATE = COL_SB_Z + SB_WIDTH
IN_COLS = COL_GATE + 2 * D_MODEL

LANES = 128
MXU_DIM = 256
ROW_BLOCK = 128
LOG2E = math.log2(math.e)
SOFTPLUS_LINEAR_ABOVE = 64.0
VMEM_LIMIT = 56 * 1024 * 1024

NT_DIMS = (((1,), (1,)), ((), ()))


def _row_blocks(n_rows):
    rb = min(ROW_BLOCK, n_rows)
    return [slice(r, r + rb) for r in range(0, n_rows, rb)]


def _chunk(j, size):
    if isinstance(j, int):
        return pl.ds(j * size, size)
    return pl.ds(pl.multiple_of(j * size, size), size)


def _params(*sem):
    return pltpu.CompilerParams(dimension_semantics=sem, vmem_limit_bytes=VMEM_LIMIT)


def _norm_kernel(x_ref, g_ref, o_ref):
    x = x_ref[...]
    ms = jnp.mean(x * x, axis=-1, keepdims=True)
    o_ref[...] = (x * lax.rsqrt(ms + RMS_EPS) * g_ref[...]).astype(o_ref.dtype)


def _rms_norm(x, gain, tm):
    s, d = x.shape
    return pl.pallas_call(
        _norm_kernel,
        grid=(s // tm,),
        in_specs=[pl.BlockSpec((tm, d), lambda i: (i, 0)),
                  pl.BlockSpec((1, d), lambda i: (0, 0))],
        out_specs=pl.BlockSpec((tm, d), lambda i: (i, 0)),
        out_shape=jax.ShapeDtypeStruct((s, d), BF16),
        compiler_params=_params("parallel"),
        name="rms_norm",
    )(x, gain.reshape(1, d))


def _cast_weights(w_ref, wb_ref):
    @pl.when(pl.program_id(1) == 0)
    def _():
        for rows in _row_blocks(w_ref.shape[0]):
            wb_ref[rows, :] = w_ref[rows, :].astype(wb_ref.dtype)


def _proj_qk_kernel(h_ref, w_ref, g_ref, cos_ref, sin_ref, o_ref, wb_ref):
    _cast_weights(w_ref, wb_ref)
    g = g_ref[...]
    w = wb_ref[...]
    for rows in _row_blocks(h_ref.shape[0]):
        acc = jnp.dot(h_ref[rows, :], w, preferred_element_type=F32)
        cos = cos_ref[rows, :]
        sin = sin_ref[rows, :]
        for t in range(o_ref.shape[0]):
            xh = acc[:, t * LANES:(t + 1) * LANES]
            ms = jnp.mean(xh * xh, axis=-1, keepdims=True)
            y = xh * lax.rsqrt(ms + RMS_EPS) * g
            y = y * cos + pltpu.roll(y, LANES // 2, 1) * sin
            o_ref[t, rows, :] = y.astype(o_ref.dtype)


def _proj_heads_kernel(h_ref, w_ref, o_ref, wb_ref, *, n_scaled, scale):
    _cast_weights(w_ref, wb_ref)
    w = wb_ref[...]
    hw = o_ref.shape[2]
    for rows in _row_blocks(h_ref.shape[0]):
        acc = jnp.dot(h_ref[rows, :], w, preferred_element_type=F32)
        if n_scaled:
            acc = acc * jnp.where(pl.program_id(0) < n_scaled, scale, 1.0).astype(F32)
        for t in range(o_ref.shape[0]):
            o_ref[t, rows, :] = acc[:, t * hw:(t + 1) * hw].astype(o_ref.dtype)


def _proj_act_kernel(h_ref, w_ref, o_ref, wb_ref, *, act):
    _cast_weights(w_ref, wb_ref)
    w = wb_ref[...]
    for rows in _row_blocks(h_ref.shape[0]):
        acc = jnp.dot(h_ref[rows, :], w, preferred_element_type=F32)
        o_ref[rows, :] = act(acc).astype(o_ref.dtype)


def _proj_call(kernel, h, w_in, layer, col_block_fn, n_tiles, tm, tn, extra_in, extra_specs,
               out_shape, out_spec, name):
    s, d = h.shape
    return pl.pallas_call(
        kernel,
        grid=(n_tiles, s // tm),
        in_specs=[pl.BlockSpec((tm, d), lambda j, i: (i, 0)),
                  pl.BlockSpec((None, d, tn), lambda j, i: (layer, 0, col_block_fn(j)))]
        + extra_specs,
        out_specs=out_spec,
        out_shape=out_shape,
        scratch_shapes=[pltpu.VMEM((d, tn), BF16)],
        compiler_params=_params("parallel", "arbitrary"),
        name=name,
    )(h, w_in, *extra_in)


def _da_kernel(sc_ref, lq1_ref, lk1_ref, lq2_ref, lk2_ref,
               q1_ref, q2_ref, k1_ref, k2_ref, v_ref, z_ref, g_ref, o_ref,
               m_ref, l_ref, acc_ref, *, tq, tk, tkl, rbs, unroll):
    i = pl.program_id(1)
    n_sub = tq // tk
    n_subl = tq // tkl
    assert unroll % n_subl == 0 and tk % rbs[1] == 0
    q_refs = (q1_ref, q2_ref)
    k_refs = (k1_ref, k2_ref)

    m_ref[...] = jnp.full(m_ref.shape, -jnp.inf, F32)
    l_ref[...] = jnp.zeros(l_ref.shape, F32)
    acc_ref[...] = jnp.zeros(acc_ref.shape, F32)

    def step(j, sub):
        w = tkl if sub is None else tk
        ks = _chunk(j, w)
        v = v_ref[ks, :]
        rb = rbs[0] if sub is None else rbs[1]
        for r0 in range(0, tq, rb):
            c0 = 0 if sub is None else sub * tk
            if sub is not None and r0 + rb <= c0:
                continue
            masked = sub is not None and r0 < c0 + tk - 1
            rows = slice(r0, r0 + rb)
            if masked:
                row = lax.broadcasted_iota(jnp.int32, (rb, tk), 0) + r0
                col = lax.broadcasted_iota(jnp.int32, (rb, tk), 1) + c0
                causal = col <= row
            for st in range(2):
                s = lax.dot_general(q_refs[st][rows, :], k_refs[st][ks, :], NT_DIMS,
                                    preferred_element_type=F32)
                if masked:
                    s = jnp.where(causal, s, MASK_VALUE)
                sg = [s[:, g * LANES:(g + 1) * LANES] for g in range(w // LANES)]
                gmax = functools.reduce(jnp.maximum, sg)
                m_old = m_ref[st, rows, :]
                m_new = jnp.maximum(m_old, jnp.max(gmax, axis=-1, keepdims=True))
                alpha = jnp.exp2(m_old - m_new)
                pg = [jnp.exp2(x - m_new) for x in sg]
                l_ref[st, rows, :] = alpha * l_ref[st, rows, :] + functools.reduce(jnp.add, pg)
                p = jnp.concatenate([x.astype(BF16) for x in pg], axis=1)
                pv = jnp.dot(p, v, preferred_element_type=F32)
                alpha_v = jnp.concatenate([alpha] * (DA_V_DIM // LANES), axis=1)
                acc_ref[st, rows, :] = alpha_v * acc_ref[st, rows, :] + pv
                m_ref[st, rows, :] = m_new

    n_full = i * n_subl
    n_main = n_full // unroll

    def body(jo, c):
        for u in range(unroll):
            step(jo * unroll + u, None)
        return c

    lax.fori_loop(0, n_main, body, 0)
    for rem in range(n_subl, unroll, n_subl):
        @pl.when(n_full - n_main * unroll == rem)
        def _(rem=rem):
            for u in range(rem):
                step(n_main * unroll + u, None)
    for sub in range(n_sub):
        step(i * n_sub + sub, sub)

    lambda_init = sc_ref[0]
    lam = (jnp.exp(jnp.sum(lq1_ref[...] * lk1_ref[...], axis=-1, keepdims=True))
           - jnp.exp(jnp.sum(lq2_ref[...] * lk2_ref[...], axis=-1, keepdims=True))
           + lambda_init)
    l1 = jnp.sum(l_ref[0], axis=-1, keepdims=True)
    l2 = jnp.sum(l_ref[1], axis=-1, keepdims=True)
    o = acc_ref[0] / l1 - lam * (acc_ref[1] / l2)
    ms = jnp.mean(o * o, axis=-1, keepdims=True)
    o = o * lax.rsqrt(ms + RMS_EPS) * g_ref[...] * (1.0 - lambda_init)
    o_ref[...] = (o * z_ref[...]).astype(o_ref.dtype)


def _diff_attention(scalars, lams, qk, v, zact, subln_gain, tq, tk, tkl, rbs, unroll):
    _, s, _ = qk.shape
    nh = DA_HEADS
    row = lambda h, i: (0, 0)
    kern = functools.partial(_da_kernel, tq=tq, tk=tk, tkl=tkl, rbs=rbs, unroll=unroll)
    return pl.pallas_call(
        kern,
        grid=(nh, s // tq),
        in_specs=[pl.BlockSpec(memory_space=pltpu.SMEM)]
        + [pl.BlockSpec((1, DA_QK_DIM), row)] * 4
        + [pl.BlockSpec((None, tq, DA_QK_DIM), lambda h, i: (2 * h, i, 0)),
           pl.BlockSpec((None, tq, DA_QK_DIM), lambda h, i: (2 * h + 1, i, 0)),
           pl.BlockSpec((None, s, DA_QK_DIM), lambda h, i: (2 * nh + 2 * h, 0, 0)),
           pl.BlockSpec((None, s, DA_QK_DIM), lambda h, i: (2 * nh + 2 * h + 1, 0, 0)),
           pl.BlockSpec((None, s, DA_V_DIM), lambda h, i: (h, 0, 0)),
           pl.BlockSpec((tq, DA_V_DIM), lambda h, i: (i, h)),
           pl.BlockSpec((1, DA_V_DIM), row)],
        out_specs=pl.BlockSpec((tq, DA_V_DIM), lambda h, i: (i, h)),
        out_shape=jax.ShapeDtypeStruct((s, DA_WIDTH), BF16),
        scratch_shapes=[pltpu.VMEM((2, tq, LANES), F32),
                        pltpu.VMEM((2, tq, LANES), F32),
                        pltpu.VMEM((2, tq, DA_V_DIM), F32)],
        compiler_params=_params("parallel", "arbitrary"),
        name="diff_attention",
    )(scalars, *lams, qk, qk, qk, qk, v, zact, subln_gain.reshape(1, DA_V_DIM))


def _sb_kernel(q_ref, k_ref, v_ref, z_ref, o_ref, c_ref, acc_ref, *, tq, tk, cw, rbs, unroll):
    i = pl.program_id(1)
    n_sub = tq // tk
    assert unroll % n_sub == 0
    tri = (lax.broadcasted_iota(jnp.int32, (cw, cw), 0)
           >= lax.broadcasted_iota(jnp.int32, (cw, cw), 1)).astype(BF16)

    c_ref[...] = jnp.zeros(c_ref.shape, F32)
    acc_ref[...] = jnp.zeros(acc_ref.shape, F32)

    def step(j, sub):
        ks = _chunk(j, tk)
        k = k_ref[ks, :]
        v = v_ref[ks, :]
        rb = rbs[0] if sub is None else rbs[1]
        for r0 in range(0, tq, rb):
            c0 = 0 if sub is None else sub * tk
            if sub is not None and r0 + rb - 1 <= c0:
                continue
            masked = sub is not None and r0 < c0 + tk
            rows = slice(r0, r0 + rb)
            z = lax.dot_general(q_ref[rows, :], k, NT_DIMS, preferred_element_type=F32)
            sp = jnp.where(z > SOFTPLUS_LINEAR_ABOVE, z, jnp.log2(1.0 + jnp.exp2(z)))
            if masked:
                row = lax.broadcasted_iota(jnp.int32, (rb, tk), 0) + r0
                col = lax.broadcasted_iota(jnp.int32, (rb, tk), 1) + c0
                strict = col < row
                sp = jnp.where(strict, sp, 0.0)
            c = c_ref[rows, :]
            ws = []
            for b in reversed(range(tk // cw)):
                blk = slice(b * cw, (b + 1) * cw)
                cs = jnp.dot(sp[:, blk].astype(BF16), tri, preferred_element_type=F32)
                w = jnp.exp2(z[:, blk] - cs - c)
                if masked:
                    w = jnp.where(strict[:, blk], w, 0.0)
                ws.append(w.astype(BF16))
                c = c + cs[:, 0:1]
            w = jnp.concatenate(ws[::-1], axis=1)
            acc_ref[rows, :] += jnp.dot(w, v, preferred_element_type=F32)
            c_ref[rows, :] = c

    for sub in reversed(range(n_sub)):
        step(i * n_sub + sub, sub)

    n_full = i * n_sub
    n_main = n_full // unroll

    def body(jo, c):
        for u in range(unroll):
            step(n_full - 1 - (jo * unroll + u), None)
        return c

    lax.fori_loop(0, n_main, body, 0)
    for rem in range(n_sub, unroll, n_sub):
        @pl.when(n_full - n_main * unroll == rem)
        def _(rem=rem):
            for u in range(rem):
                step(rem - 1 - u, None)
    o_ref[...] = (acc_ref[...] * z_ref[...]).astype(o_ref.dtype)


def _sb_attention(qkv, zact, tq, tk, cw, rbs, unroll):
    _, s, _ = qkv.shape
    nh = SB_HEADS
    hd = SB_HEAD_DIM
    z_off = DA_WIDTH // hd
    kern = functools.partial(_sb_kernel, tq=tq, tk=tk, cw=cw, rbs=rbs, unroll=unroll)
    return pl.pallas_call(
        kern,
        grid=(nh, s // tq),
        in_specs=[pl.BlockSpec((None, tq, hd), lambda h, i: (h, i, 0)),
                  pl.BlockSpec((None, s, hd), lambda h, i: (nh + h, 0, 0)),
                  pl.BlockSpec((None, s, hd), lambda h, i: (2 * nh + h, 0, 0)),
                  pl.BlockSpec((tq, hd), lambda h, i: (i, z_off + h))],
        out_specs=pl.BlockSpec((tq, hd), lambda h, i: (i, h)),
        out_shape=jax.ShapeDtypeStruct((s, SB_WIDTH), BF16),
        scratch_shapes=[pltpu.VMEM((tq, 1), F32),
                        pltpu.VMEM((tq, hd), F32)],
        compiler_params=_params("parallel", "arbitrary"),
        name="sb_attention",
    )(qkv, qkv, qkv, zact)


def _merge_kernel(ua_ref, ub_ref, wa_ref, wb_ref, ga_ref, gb_ref, o_ref):
    wa = wa_ref[...]
    wb = wb_ref[...]
    for rows in _row_blocks(ua_ref.shape[0]):
        ya = jnp.dot(ua_ref[rows, :], wa, preferred_element_type=F32)
        yb = jnp.dot(ub_ref[rows, :], wb, preferred_element_type=F32)
        o_ref[rows, :] = (ga_ref[rows, :] * ya + gb_ref[rows, :] * yb).astype(o_ref.dtype)


def _merge(ua, ub, wa, wb, gates, layer, tm, tn):
    s, d = ua.shape
    nj = D_MODEL // tn
    return pl.pallas_call(
        _merge_kernel,
        grid=(s // tm, nj),
        in_specs=[pl.BlockSpec((tm, d), lambda i, j: (i, 0)),
                  pl.BlockSpec((tm, d), lambda i, j: (i, 0)),
                  pl.BlockSpec((None, d, tn), lambda i, j: (layer, 0, j)),
                  pl.BlockSpec((None, d, tn), lambda i, j: (layer, 0, j)),
                  pl.BlockSpec((tm, tn), lambda i, j: (i, j)),
                  pl.BlockSpec((tm, tn), lambda i, j: (i, nj + j))],
        out_specs=pl.BlockSpec((tm, tn), lambda i, j: (i, j)),
        out_shape=jax.ShapeDtypeStruct((s, D_MODEL), BF16),
        compiler_params=_params("parallel", "arbitrary"),
        name="gated_merge",
    )(ua, ub, wa, wb, gates, gates)


def _out_kernel(y_ref, w_ref, x_ref, *rest):
    if len(rest) == 3:
        g_ref, o_ref, h_ref = rest
        g = g_ref[...]
    else:
        (o_ref,) = rest
        h_ref = None
    w = w_ref[...]
    for rows in _row_blocks(y_ref.shape[0]):
        o = x_ref[rows, :] + jnp.dot(y_ref[rows, :], w, preferred_element_type=F32)
        o_ref[rows, :] = o
        if h_ref is not None:
            ms = jnp.mean(o * o, axis=-1, keepdims=True)
            h_ref[rows, :] = (o * lax.rsqrt(ms + RMS_EPS) * g).astype(h_ref.dtype)


def _out_proj(y, w_out, x, layer, next_gain, tm):
    s, d = y.shape
    in_specs = [pl.BlockSpec((tm, d), lambda i: (i, 0)),
                pl.BlockSpec((None, d, d), lambda i: (layer, 0, 0)),
                pl.BlockSpec((tm, d), lambda i: (i, 0))]
    out_specs = pl.BlockSpec((tm, d), lambda i: (i, 0))
    out_shape = jax.ShapeDtypeStruct((s, d), F32)
    args = [y, w_out, x]
    if next_gain is not None:
        in_specs.append(pl.BlockSpec((1, d), lambda i: (0, 0)))
        out_specs = [out_specs, pl.BlockSpec((tm, d), lambda i: (i, 0))]
        out_shape = [out_shape, jax.ShapeDtypeStruct((s, d), BF16)]
        args.append(next_gain.reshape(1, d))
    return pl.pallas_call(
        _out_kernel,
        grid=(s // tm,),
        in_specs=in_specs,
        out_specs=out_specs,
        out_shape=out_shape,
        compiler_params=_params("parallel"),
        name="out_proj",
    )(*args)


def _silu(x):
    return x * (1.0 / (1.0 + jnp.exp(-x)))


def _sigmoid(x):
    return 1.0 / (1.0 + jnp.exp(-x))


def _tiles(s):
    tm = min(s, 1024)
    tq = min(s, 2048)
    tk = min(tq, 512)
    da_tq = tq
    sb_tq = tq
    da_tkl = tk
    return dict(tm=tm, tn=512, tp=1024, to=min(s, 512), da_tq=da_tq, sb_tq=sb_tq,
                da_tk=tk, da_tkl=da_tkl, sb_tk=tk, sb_cw=min(tk, MXU_DIM),
                da_rb=(min(tq, 1024), min(tq, 512)), sb_rb=(min(sb_tq, 1024), min(sb_tq, 1024)),
                da_unroll=da_tq // da_tkl, sb_unroll=sb_tq // tk)


def kernel(x, norm_gain, w_in, qk_q_gain, qk_k_gain, lambda_q1, lambda_k1, lambda_q2, lambda_k2,
           subln_gain, w_branch_a, w_branch_b, w_out):
    b, s, d = x.shape
    assert b == 1 and d == D_MODEL and w_in.shape[2] == IN_COLS
    depth = w_in.shape[0]
    t = _tiles(s)
    tm, tn, tp = t["tm"], t["tn"], t["tp"]
    assert s % tm == 0 and s % t["da_tq"] == 0 and s % t["sb_tq"] == 0

    wa_b = w_branch_a.astype(BF16)
    wb_b = w_branch_b.astype(BF16)
    wo_b = w_out.astype(BF16)

    hd = DA_QK_DIM
    inv_freq = jnp.exp(-(jnp.arange(0, hd, 2, dtype=F32) / hd) * math.log(ROPE_THETA))
    ang = jnp.arange(s, dtype=F32)[:, None] * inv_freq[None, :]
    cos2 = jnp.concatenate([jnp.cos(ang), jnp.cos(ang)], axis=-1)
    sin2 = jnp.concatenate([-jnp.sin(ang), jnp.sin(ang)], axis=-1)

    xs = x.reshape(s, d)
    h = _rms_norm(xs, norm_gain[0], tm)
    for l in range(depth):
        lambda_init = 0.8 - 0.6 * math.exp(-0.3 * l)

        nqk = 2 * DA_HEADS * DA_QK_DIM // tp
        gains = jnp.stack([qk_q_gain[l] * (DA_QK_DIM ** -0.5 * LOG2E),
                           qk_k_gain[l]]).reshape(2, 1, hd)
        qk = _proj_call(
            _proj_qk_kernel, h, w_in, l, lambda j: COL_DA_Q // tp + j, 2 * nqk, tm, tp,
            [gains, cos2, sin2],
            [pl.BlockSpec((None, 1, hd), lambda j, i: (j // nqk, 0, 0)),
             pl.BlockSpec((tm, hd), lambda j, i: (i, 0)),
             pl.BlockSpec((tm, hd), lambda j, i: (i, 0))],
            jax.ShapeDtypeStruct((4 * DA_HEADS, s, hd), BF16),
            pl.BlockSpec((tp // hd, tm, hd), lambda j, i: (j, i, 0)),
            "proj_da_qk")
        v_da = _proj_call(
            functools.partial(_proj_heads_kernel, n_scaled=0, scale=1.0),
            h, w_in, l, lambda j: COL_DA_V // tp + j, DA_WIDTH // tp, tm, tp, [], [],
            jax.ShapeDtypeStruct((DA_HEADS, s, DA_V_DIM), BF16),
            pl.BlockSpec((tp // DA_V_DIM, tm, DA_V_DIM), lambda j, i: (j, i, 0)),
            "proj_da_v")
        qkv_sb = _proj_call(
            functools.partial(_proj_heads_kernel, n_scaled=SB_WIDTH // tp,
                              scale=SB_HEAD_DIM ** -0.5 * LOG2E),
            h, w_in, l, lambda j: COL_SB_Q // tp + j, 3 * SB_WIDTH // tp, tm, tp, [], [],
            jax.ShapeDtypeStruct((3 * SB_HEADS, s, SB_HEAD_DIM), BF16),
            pl.BlockSpec((tp // SB_HEAD_DIM, tm, SB_HEAD_DIM), lambda j, i: (j, i, 0)),
            "proj_sb_qkv")
        nz = DA_WIDTH // tp
        zact = _proj_call(
            functools.partial(_proj_act_kernel, act=_silu),
            h, w_in, l,
            lambda j: jnp.where(j < nz, COL_DA_Z // tp + j, COL_SB_Z // tp + (j - nz)),
            nz + SB_WIDTH // tp, tm, tp, [], [],
            jax.ShapeDtypeStruct((s, DA_WIDTH + SB_WIDTH), BF16),
            pl.BlockSpec((tm, tp), lambda j, i: (i, j)),
            "proj_silu")
        gates = _proj_call(
            functools.partial(_proj_act_kernel, act=_sigmoid),
            h, w_in, l, lambda j: COL_GATE // tp + j, 2 * D_MODEL // tp, tm, tp, [], [],
            jax.ShapeDtypeStruct((s, 2 * D_MODEL), BF16),
            pl.BlockSpec((tm, tp), lambda j, i: (i, j)),
            "proj_gates")

        scalars = jnp.array([lambda_init], F32)
        lams = [a[l].reshape(1, hd) for a in (lambda_q1, lambda_k1, lambda_q2, lambda_k2)]
        ua = _diff_attention(scalars, lams, qk, v_da, zact, subln_gain[l], t["da_tq"],
                             t["da_tk"], t["da_tkl"], t["da_rb"], t["da_unroll"])
        ub = _sb_attention(qkv_sb, zact, t["sb_tq"], t["sb_tk"], t["sb_cw"], t["sb_rb"],
                           t["sb_unroll"])

        y = _merge(ua, ub, wa_b, wb_b, gates, l, tm, tn)
        if l + 1 < depth:
            xs, h = _out_proj(y, wo_b, xs, l, norm_gain[l + 1], t["to"])
        else:
            xs = _out_proj(y, wo_b, xs, l, None, t["to"])
    return xs.reshape(b, s, d)
```

```python
import functools
import math

import jax
import jax.numpy as jnp
from jax import lax
from jax.experimental import pallas as pl
from jax.experimental.pallas import tpu as pltpu

F32 = jnp.float32
BF16 = jnp.bfloat16

D_MODEL = 2048
RMS_EPS = 1e-6
ROPE_THETA = 10000.0
MASK_VALUE = -1e30

DA_HEADS = 8
DA_QK_DIM = 128
DA_V_DIM = 2 * DA_QK_DIM
DA_WIDTH = DA_HEADS * DA_V_DIM
SB_HEADS = 16
SB_HEAD_DIM = 128
SB_WIDTH = SB_HEADS * SB_HEAD_DIM

COL_DA_Q = 0
COL_DA_K = COL_DA_Q + 2 * DA_HEADS * DA_QK_DIM
COL_DA_V = COL_DA_K + 2 * DA_HEADS * DA_QK_DIM
COL_DA_Z = COL_DA_V + DA_WIDTH
COL_SB_Q = COL_DA_Z + DA_WIDTH
COL_SB_K = COL_SB_Q + SB_WIDTH
COL_SB_V = COL_SB_K + SB_WIDTH
COL_SB_Z = COL_SB_V + SB_WIDTH
COL_GATE = COL_SB_Z + SB_WIDTH
IN_COLS = COL_GATE + 2 * D_MODEL

LANES = 128
MXU_DIM = 256
ROW_BLOCK = 128
LOG2E = math.log2(math.e)
SOFTPLUS_LINEAR_ABOVE = 64.0
VMEM_LIMIT = 56 * 1024 * 1024

NT_DIMS = (((1,), (1,)), ((), ()))


def _row_blocks(n_rows):
    rb = min(ROW_BLOCK, n_rows)
    return [slice(r, r + rb) for r in range(0, n_rows, rb)]


def _chunk(j, size):
    if isinstance(j, int):
        return pl.ds(j * size, size)
    return pl.ds(pl.multiple_of(j * size, size), size)


def _params(*sem):
    return pltpu.CompilerParams(dimension_semantics=sem, vmem_limit_bytes=VMEM_LIMIT)


def _norm_kernel(x_ref, g_ref, o_ref):
    x = x_ref[...]
    ms = jnp.mean(x * x, axis=-1, keepdims=True)
    o_ref[...] = (x * lax.rsqrt(ms + RMS_EPS) * g_ref[...]).astype(o_ref.dtype)


def _rms_norm(x, gain, tm):
    s, d = x.shape
    return pl.pallas_call(
        _norm_kernel,
        grid=(s // tm,),
        in_specs=[pl.BlockSpec((tm, d), lambda i: (i, 0)),
                  pl.BlockSpec((1, d), lambda i: (0, 0))],
        out_specs=pl.BlockSpec((tm, d), lambda i: (i, 0)),
        out_shape=jax.ShapeDtypeStruct((s, d), BF16),
        compiler_params=_params("parallel"),
        name="rms_norm",
    )(x, gain.reshape(1, d))


def _cast_weights(w_ref, wb_ref):
    @pl.when(pl.program_id(1) == 0)
    def _():
        for rows in _row_blocks(w_ref.shape[0]):
            wb_ref[rows, :] = w_ref[rows, :].astype(wb_ref.dtype)


def _proj_qk_kernel(h_ref, w_ref, g_ref, cos_ref, sin_ref, o_ref, wb_ref):
    _cast_weights(w_ref, wb_ref)
    g = g_ref[...]
    w = wb_ref[...]
    for rows in _row_blocks(h_ref.shape[0]):
        acc = jnp.dot(h_ref[rows, :], w, preferred_element_type=F32)
        cos = cos_ref[rows, :]
        sin = sin_ref[rows, :]
        for t in range(o_ref.shape[0]):
            xh = acc[:, t * LANES:(t + 1) * LANES]
            ms = jnp.mean(xh * xh, axis=-1, keepdims=True)
            y = xh * lax.rsqrt(ms + RMS_EPS) * g
            y = y * cos + pltpu.roll(y, LANES // 2, 1) * sin
            o_ref[t, rows, :] = y.astype(o_ref.dtype)


def _proj_heads_kernel(h_ref, w_ref, o_ref, wb_ref, *, n_scaled, scale):
    _cast_weights(w_ref, wb_ref)
    w = wb_ref[...]
    hw = o_ref.shape[2]
    for rows in _row_blocks(h_ref.shape[0]):
        acc = jnp.dot(h_ref[rows, :], w, preferred_element_type=F32)
        if n_scaled:
            acc = acc * jnp.where(pl.program_id(0) < n_scaled, scale, 1.0).astype(F32)
        for t in range(o_ref.shape[0]):
            o_ref[t, rows, :] = acc[:, t * hw:(t + 1) * hw].astype(o_ref.dtype)


def _proj_act_kernel(h_ref, w_ref, o_ref, wb_ref, *, act):
    _cast_weights(w_ref, wb_ref)
    w = wb_ref[...]
    for rows in _row_blocks(h_ref.shape[0]):
        acc = jnp.dot(h_ref[rows, :], w, preferred_element_type=F32)
        o_ref[rows, :] = act(acc).astype(o_ref.dtype)


def _proj_call(kernel, h, w_in, layer, col_block_fn, n_tiles, tm, tn, extra_in, extra_specs,
               out_shape, out_spec, name):
    s, d = h.shape
    return pl.pallas_call(
        kernel,
        grid=(n_tiles, s // tm),
        in_specs=[pl.BlockSpec((tm, d), lambda j, i: (i, 0)),
                  pl.BlockSpec((None, d, tn), lambda j, i: (layer, 0, col_block_fn(j)))]
        + extra_specs,
        out_specs=out_spec,
        out_shape=out_shape,
        scratch_shapes=[pltpu.VMEM((d, tn), BF16)],
        compiler_params=_params("parallel", "arbitrary"),
        name=name,
    )(h, w_in, *extra_in)


def _da_kernel(sc_ref, lq1_ref, lk1_ref, lq2_ref, lk2_ref,
               q1_ref, q2_ref, k1_ref, k2_ref, v_ref, z_ref, g_ref, o_ref,
               m_ref, l_ref, acc_ref, *, tq, tk, tkl, rbs, unroll):
    i = pl.program_id(1)
    n_sub = tq // tk
    n_subl = tq // tkl
    assert unroll % n_subl == 0 and tk % rbs[1] == 0
    q_refs = (q1_ref, q2_ref)
    k_refs = (k1_ref, k2_ref)

    m_ref[...] = jnp.full(m_ref.shape, -jnp.inf, F32)
    l_ref[...] = jnp.zeros(l_ref.shape, F32)
    acc_ref[...] = jnp.zeros(acc_ref.shape, F32)

    def step(j, sub):
        w = tkl if sub is None else tk
        ks = _chunk(j, w)
        v = v_ref[ks, :]
        rb = rbs[0] if sub is None else rbs[1]
        for r0 in range(0, tq, rb):
            c0 = 0 if sub is None else sub * tk
            if sub is not None and r0 + rb <= c0:
                continue
            masked = sub is not None and r0 < c0 + tk - 1
            rows = slice(r0, r0 + rb)
            if masked:
                row = lax.broadcasted_iota(jnp.int32, (rb, tk), 0) + r0
                col = lax.broadcasted_iota(jnp.int32, (rb, tk), 1) + c0
                causal = col <= row
            for st in range(2):
                s = lax.dot_general(q_refs[st][rows, :], k_refs[st][ks, :], NT_DIMS,
                                    preferred_element_type=F32)
                if masked:
                    s = jnp.where(causal, s, MASK_VALUE)
                sg = [s[:, g * LANES:(g + 1) * LANES] for g in range(w // LANES)]
                gmax = functools.reduce(jnp.maximum, sg)
                m_old = m_ref[st, rows, :]
                m_new = jnp.maximum(m_old, jnp.max(gmax, axis=-1, keepdims=True))
                alpha = jnp.exp2(m_old - m_new)
                pg = [jnp.exp2(x - m_new) for x in sg]
                l_ref[st, rows, :] = alpha * l_ref[st, rows, :] + functools.reduce(jnp.add, pg)
                p = jnp.concatenate([x.astype(BF16) for x in pg], axis=1)
                pv = jnp.dot(p, v, preferred_element_type=F32)
                alpha_v = jnp.concatenate([alpha] * (DA_V_DIM // LANES), axis=1)
                acc_ref[st, rows, :] = alpha_v * acc_ref[st, rows, :] + pv
                m_ref[st, rows, :] = m_new

    n_full = i * n_subl
    n_main = n_full // unroll

    def body(jo, c):
        for u in range(unroll):
            step(jo * unroll + u, None)
        return c

    lax.fori_loop(0, n_main, body, 0)
    for rem in range(n_subl, unroll, n_subl):
        @pl.when(n_full - n_main * unroll == rem)
        def _(rem=rem):
            for u in range(rem):
                step(n_main * unroll + u, None)
    for sub in range(n_sub):
        step(i * n_sub + sub, sub)

    lambda_init = sc_ref[0]
    lam = (jnp.exp(jnp.sum(lq1_ref[...] * lk1_ref[...], axis=-1, keepdims=True))
           - jnp.exp(jnp.sum(lq2_ref[...] * lk2_ref[...], axis=-1, keepdims=True))
           + lambda_init)
    l1 = jnp.sum(l_ref[0], axis=-1, keepdims=True)
    l2 = jnp.sum(l_ref[1], axis=-1, keepdims=True)
    o = acc_ref[0] / l1 - lam * (acc_ref[1] / l2)
    ms = jnp.mean(o * o, axis=-1, keepdims=True)
    o = o * lax.rsqrt(ms + RMS_EPS) * g_ref[...] * (1.0 - lambda_init)
    o_ref[...] = (o * z_ref[...]).astype(o_ref.dtype)


def _diff_attention(scalars, lams, qk, v, zact, subln_gain, tq, tk, tkl, rbs, unroll):
    _, s, _ = qk.shape
    nh = DA_HEADS
    row = lambda h, i: (0, 0)
    kern = functools.partial(_da_kernel, tq=tq, tk=tk, tkl=tkl, rbs=rbs, unroll=unroll)
    return pl.pallas_call(
        kern,
        grid=(nh, s // tq),
        in_specs=[pl.BlockSpec(memory_space=pltpu.SMEM)]
        + [pl.BlockSpec((1, DA_QK_DIM), row)] * 4
        + [pl.BlockSpec((None, tq, DA_QK_DIM), lambda h, i: (2 * h, i, 0)),
           pl.BlockSpec((None, tq, DA_QK_DIM), lambda h, i: (2 * h + 1, i, 0)),
           pl.BlockSpec((None, s, DA_QK_DIM), lambda h, i: (2 * nh + 2 * h, 0, 0)),
           pl.BlockSpec((None, s, DA_QK_DIM), lambda h, i: (2 * nh + 2 * h + 1, 0, 0)),
           pl.BlockSpec((None, s, DA_V_DIM), lambda h, i: (h, 0, 0)),
           pl.BlockSpec((tq, DA_V_DIM), lambda h, i: (i, h)),
           pl.BlockSpec((1, DA_V_DIM), row)],
        out_specs=pl.BlockSpec((tq, DA_V_DIM), lambda h, i: (i, h)),
        out_shape=jax.ShapeDtypeStruct((s, DA_WIDTH), BF16),
        scratch_shapes=[pltpu.VMEM((2, tq, LANES), F32),
                        pltpu.VMEM((2, tq, LANES), F32),
                        pltpu.VMEM((2, tq, DA_V_DIM), F32)],
        compiler_params=_params("parallel", "arbitrary"),
        name="diff_attention",
    )(scalars, *lams, qk, qk, qk, qk, v, zact, subln_gain.reshape(1, DA_V_DIM))


def _sb_kernel(q_ref, k_ref, v_ref, z_ref, o_ref, c_ref, acc_ref, *, tq, tk, cw, rbs, unroll):
    i = pl.program_id(1)
    n_sub = tq // tk
    assert unroll % n_sub == 0
    tri = (lax.broadcasted_iota(jnp.int32, (cw, cw), 0)
           >= lax.broadcasted_iota(jnp.int32, (cw, cw), 1)).astype(BF16)

    c_ref[...] = jnp.zeros(c_ref.shape, F32)
    acc_ref[...] = jnp.zeros(acc_ref.shape, F32)

    def step(j, sub):
        ks = _chunk(j, tk)
        k = k_ref[ks, :]
        v = v_ref[ks, :]
        rb = rbs[0] if sub is None else rbs[1]
        for r0 in range(0, tq, rb):
            c0 = 0 if sub is None else sub * tk
            if sub is not None and r0 + rb - 1 <= c0:
                continue
            masked = sub is not None and r0 < c0 + tk
            rows = slice(r0, r0 + rb)
            z = lax.dot_general(q_ref[rows, :], k, NT_DIMS, preferred_element_type=F32)
            sp = jnp.where(z > SOFTPLUS_LINEAR_ABOVE, z, jnp.log2(1.0 + jnp.exp2(z)))
            if masked:
                row = lax.broadcasted_iota(jnp.int32, (rb, tk), 0) + r0
                col = lax.broadcasted_iota(jnp.int32, (rb, tk), 1) + c0
                strict = col < row
                sp = jnp.where(strict, sp, 0.0)
            c = c_ref[rows, :]
            ws = []
            for b in reversed(range(tk // cw)):
                blk = slice(b * cw, (b + 1) * cw)
                cs = jnp.dot(sp[:, blk].astype(BF16), tri, preferred_element_type=F32)
                w = jnp.exp2(z[:, blk] - cs - c)
                if masked:
                    w = jnp.where(strict[:, blk], w, 0.0)
                ws.append(w.astype(BF16))
                c = c + cs[:, 0:1]
            w = jnp.concatenate(ws[::-1], axis=1)
            acc_ref[rows, :] += jnp.dot(w, v, preferred_element_type=F32)
            c_ref[rows, :] = c

    for sub in reversed(range(n_sub)):
        step(i * n_sub + sub, sub)

    n_full = i * n_sub
    n_main = n_full // unroll

    def body(jo, c):
        for u in range(unroll):
            step(n_full - 1 - (jo * unroll + u), None)
        return c

    lax.fori_loop(0, n_main, body, 0)
    for rem in range(n_sub, unroll, n_sub):
        @pl.when(n_full - n_main * unroll == rem)
        def _(rem=rem):
            for u in range(rem):
                step(rem - 1 - u, None)
    o_ref[...] = (acc_ref[...] * z_ref[...]).astype(o_ref.dtype)


def _sb_attention(qkv, zact, tq, tk, cw, rbs, unroll):
    _, s, _ = qkv.shape
    nh = SB_HEADS
    hd = SB_HEAD_DIM
    z_off = DA_WIDTH // hd
    kern = functools.partial(_sb_kernel, tq=tq, tk=tk, cw=cw, rbs=rbs, unroll=unroll)
    return pl.pallas_call(
        kern,
        grid=(nh, s // tq),
        in_specs=[pl.BlockSpec((None, tq, hd), lambda h, i: (h, i, 0)),
                  pl.BlockSpec((None, s, hd), lambda h, i: (nh + h, 0, 0)),
                  pl.BlockSpec((None, s, hd), lambda h, i: (2 * nh + h, 0, 0)),
                  pl.BlockSpec((tq, hd), lambda h, i: (i, z_off + h))],
        out_specs=pl.BlockSpec((tq, hd), lambda h, i: (i, h)),
        out_shape=jax.ShapeDtypeStruct((s, SB_WIDTH), BF16),
        scratch_shapes=[pltpu.VMEM((tq, 1), F32),
                        pltpu.VMEM((tq, hd), F32)],
        compiler_params=_params("parallel", "arbitrary"),
        name="sb_attention",
    )(qkv, qkv, qkv, zact)


def _merge_out_kernel(ua_ref, ub_ref, wa_ref, wb_ref, ga_ref, gb_ref, wo_ref, x_ref, *rest):
    if len(rest) == 3:
        g_ref, o_ref, h_ref = rest
        g = g_ref[...]
    else:
        (o_ref,) = rest
        h_ref = None
    for rows in _row_blocks(ua_ref.shape[0]):
        ya = jnp.dot(ua_ref[rows, :], wa_ref[...], preferred_element_type=F32)
        yb = jnp.dot(ub_ref[rows, :], wb_ref[...], preferred_element_type=F32)
        y = (ga_ref[rows, :] * ya + gb_ref[rows, :] * yb).astype(BF16)
        o = x_ref[rows, :] + jnp.dot(y, wo_ref[...], preferred_element_type=F32)
        o_ref[rows, :] = o
        if h_ref is not None:
            ms = jnp.mean(o * o, axis=-1, keepdims=True)
            h_ref[rows, :] = (o * lax.rsqrt(ms + RMS_EPS) * g).astype(h_ref.dtype)


def _merge_out(ua, ub, wa, wb, gates, wo, x, layer, next_gain, tm):
    s, d = ua.shape
    rows = pl.BlockSpec((tm, d), lambda i: (i, 0))
    weight = pl.BlockSpec((None, d, d), lambda i: (layer, 0, 0), pipeline_mode=pl.Buffered(1))
    in_specs = [rows, rows, weight, weight, rows,
                pl.BlockSpec((tm, d), lambda i: (i, 1)), weight, rows]
    out_specs = rows
    out_shape = jax.ShapeDtypeStruct((s, d), F32)
    args = [ua, ub, wa, wb, gates, gates, wo, x]
    if next_gain is not None:
        in_specs.append(pl.BlockSpec((1, d), lambda i: (0, 0)))
        out_specs = [rows, rows]
        out_shape = [out_shape, jax.ShapeDtypeStruct((s, d), BF16)]
        args.append(next_gain.reshape(1, d))
    return pl.pallas_call(
        _merge_out_kernel,
        grid=(s // tm,),
        in_specs=in_specs,
        out_specs=out_specs,
        out_shape=out_shape,
        compiler_params=_params("parallel"),
        name="merge_out",
    )(*args)


def _merge_kernel(ua_ref, ub_ref, wa_ref, wb_ref, ga_ref, gb_ref, o_ref):
    wa = wa_ref[...]
    wb = wb_ref[...]
    for rows in _row_blocks(ua_ref.shape[0]):
        ya = jnp.dot(ua_ref[rows, :], wa, preferred_element_type=F32)
        yb = jnp.dot(ub_ref[rows, :], wb, preferred_element_type=F32)
        o_ref[rows, :] = (ga_ref[rows, :] * ya + gb_ref[rows, :] * yb).astype(o_ref.dtype)


def _merge(ua, ub, wa, wb, gates, layer, tm, tn):
    s, d = ua.shape
    nj = D_MODEL // tn
    return pl.pallas_call(
        _merge_kernel,
        grid=(s // tm, nj),
        in_specs=[pl.BlockSpec((tm, d), lambda i, j: (i, 0)),
                  pl.BlockSpec((tm, d), lambda i, j: (i, 0)),
                  pl.BlockSpec((None, d, tn), lambda i, j: (layer, 0, j)),
                  pl.BlockSpec((None, d, tn), lambda i, j: (layer, 0, j)),
                  pl.BlockSpec((tm, tn), lambda i, j: (i, j)),
                  pl.BlockSpec((tm, tn), lambda i, j: (i, nj + j))],
        out_specs=pl.BlockSpec((tm, tn), lambda i, j: (i, j)),
        out_shape=jax.ShapeDtypeStruct((s, D_MODEL), BF16),
        compiler_params=_params("parallel", "arbitrary"),
        name="gated_merge",
    )(ua, ub, wa, wb, gates, gates)


def _out_kernel(y_ref, w_ref, x_ref, *rest):
    if len(rest) == 3:
        g_ref, o_ref, h_ref = rest
        g = g_ref[...]
    else:
        (o_ref,) = rest
        h_ref = None
    w = w_ref[...]
    for rows in _row_blocks(y_ref.shape[0]):
        o = x_ref[rows, :] + jnp.dot(y_ref[rows, :], w, preferred_element_type=F32)
        o_ref[rows, :] = o
        if h_ref is not None:
            ms = jnp.mean(o * o, axis=-1, keepdims=True)
            h_ref[rows, :] = (o * lax.rsqrt(ms + RMS_EPS) * g).astype(h_ref.dtype)


def _out_proj(y, w_out, x, layer, next_gain, tm):
    s, d = y.shape
    in_specs = [pl.BlockSpec((tm, d), lambda i: (i, 0)),
                pl.BlockSpec((None, d, d), lambda i: (layer, 0, 0)),
                pl.BlockSpec((tm, d), lambda i: (i, 0))]
    out_specs = pl.BlockSpec((tm, d), lambda i: (i, 0))
    out_shape = jax.ShapeDtypeStruct((s, d), F32)
    args = [y, w_out, x]
    if next_gain is not None:
        in_specs.append(pl.BlockSpec((1, d), lambda i: (0, 0)))
        out_specs = [out_specs, pl.BlockSpec((tm, d), lambda i: (i, 0))]
        out_shape = [out_shape, jax.ShapeDtypeStruct((s, d), BF16)]
        args.append(next_gain.reshape(1, d))
    return pl.pallas_call(
        _out_kernel,
        grid=(s // tm,),
        in_specs=in_specs,
        out_specs=out_specs,
        out_shape=out_shape,
        compiler_params=_params("parallel"),
        name="out_proj",
    )(*args)


def _silu(x):
    return x * (1.0 / (1.0 + jnp.exp(-x)))


def _sigmoid(x):
    return 1.0 / (1.0 + jnp.exp(-x))


def _tiles(s):
    tm = min(s, 1024)
    tq = min(s, 2048)
    tk = min(tq, 512)
    da_tq = tq
    sb_tq = tq
    da_tkl = tk
    return dict(tm=tm, tn=512, tp=1024, to=min(s, 256), da_tq=da_tq, sb_tq=sb_tq,
                da_tk=tk, da_tkl=da_tkl, sb_tk=tk, sb_cw=min(tk, MXU_DIM),
                da_rb=(min(tq, 1024), min(tq, 512)), sb_rb=(min(sb_tq, 1024), min(sb_tq, 512)),
                da_unroll=da_tq // da_tkl, sb_unroll=sb_tq // tk)


def kernel(x, norm_gain, w_in, qk_q_gain, qk_k_gain, lambda_q1, lambda_k1, lambda_q2, lambda_k2,
           subln_gain, w_branch_a, w_branch_b, w_out):
    b, s, d = x.shape
    assert b == 1 and d == D_MODEL and w_in.shape[2] == IN_COLS
    depth = w_in.shape[0]
    t = _tiles(s)
    tm, tn, tp = t["tm"], t["tn"], t["tp"]
    assert s % tm == 0 and s % t["da_tq"] == 0 and s % t["sb_tq"] == 0

    wa_b = w_branch_a.astype(BF16)
    wb_b = w_branch_b.astype(BF16)
    wo_b = w_out.astype(BF16)

    hd = DA_QK_DIM
    inv_freq = jnp.exp(-(jnp.arange(0, hd, 2, dtype=F32) / hd) * math.log(ROPE_THETA))
    ang = jnp.arange(s, dtype=F32)[:, None] * inv_freq[None, :]
    cos2 = jnp.concatenate([jnp.cos(ang), jnp.cos(ang)], axis=-1)
    sin2 = jnp.concatenate([-jnp.sin(ang), jnp.sin(ang)], axis=-1)

    xs = x.reshape(s, d)
    h = _rms_norm(xs, norm_gain[0], tm)
    for l in range(depth):
        lambda_init = 0.8 - 0.6 * math.exp(-0.3 * l)

        nqk = 2 * DA_HEADS * DA_QK_DIM // tp
        gains = jnp.stack([qk_q_gain[l] * (DA_QK_DIM ** -0.5 * LOG2E),
                           qk_k_gain[l]]).reshape(2, 1, hd)
        qk = _proj_call(
            _proj_qk_kernel, h, w_in, l, lambda j: COL_DA_Q // tp + j, 2 * nqk, tm, tp,
            [gains, cos2, sin2],
            [pl.BlockSpec((None, 1, hd), lambda j, i: (j // nqk, 0, 0)),
             pl.BlockSpec((tm, hd), lambda j, i: (i, 0)),
             pl.BlockSpec((tm, hd), lambda j, i: (i, 0))],
            jax.ShapeDtypeStruct((4 * DA_HEADS, s, hd), BF16),
            pl.BlockSpec((tp // hd, tm, hd), lambda j, i: (j, i, 0)),
            "proj_da_qk")
        v_da = _proj_call(
            functools.partial(_proj_heads_kernel, n_scaled=0, scale=1.0),
            h, w_in, l, lambda j: COL_DA_V // tp + j, DA_WIDTH // tp, tm, tp, [], [],
            jax.ShapeDtypeStruct((DA_HEADS, s, DA_V_DIM), BF16),
            pl.BlockSpec((tp // DA_V_DIM, tm, DA_V_DIM), lambda j, i: (j, i, 0)),
            "proj_da_v")
        qkv_sb = _proj_call(
            functools.partial(_proj_heads_kernel, n_scaled=SB_WIDTH // tp,
                              scale=SB_HEAD_DIM ** -0.5 * LOG2E),
            h, w_in, l, lambda j: COL_SB_Q // tp + j, 3 * SB_WIDTH // tp, tm, tp, [], [],
            jax.ShapeDtypeStruct((3 * SB_HEADS, s, SB_HEAD_DIM), BF16),
            pl.BlockSpec((tp // SB_HEAD_DIM, tm, SB_HEAD_DIM), lambda j, i: (j, i, 0)),
            "proj_sb_qkv")
        nz = DA_WIDTH // tp
        zact = _proj_call(
            functools.partial(_proj_act_kernel, act=_silu),
            h, w_in, l,
            lambda j: jnp.where(j < nz, COL_DA_Z // tp + j, COL_SB_Z // tp + (j - nz)),
            nz + SB_WIDTH // tp, tm, tp, [], [],
            jax.ShapeDtypeStruct((s, DA_WIDTH + SB_WIDTH), BF16),
            pl.BlockSpec((tm, tp), lambda j, i: (i, j)),
            "proj_silu")
        gates = _proj_call(
            functools.partial(_proj_act_kernel, act=_sigmoid),
            h, w_in, l, lambda j: COL_GATE // tp + j, 2 * D_MODEL // tp, tm, tp, [], [],
            jax.ShapeDtypeStruct((s, 2 * D_MODEL), BF16),
            pl.BlockSpec((tm, tp), lambda j, i: (i, j)),
            "proj_gates")

        scalars = jnp.array([lambda_init], F32)
        lams = [a[l].reshape(1, hd) for a in (lambda_q1, lambda_k1, lambda_q2, lambda_k2)]
        ua = _diff_attention(scalars, lams, qk, v_da, zact, subln_gain[l], t["da_tq"],
                             t["da_tk"], t["da_tkl"], t["da_rb"], t["da_unroll"])
        ub = _sb_attention(qkv_sb, zact, t["sb_tq"], t["sb_tk"], t["sb_cw"], t["sb_rb"],
                           t["sb_unroll"])

        if l + 1 < depth:
            xs, h = _merge_out(ua, ub, wa_b, wb_b, gates, wo_b, xs, l, norm_gain[l + 1], t["to"])
        else:
            xs = _merge_out(ua, ub, wa_b, wb_b, gates, wo_b, xs, l, None, t["to"])
    return xs.reshape(b, s, d)
```
